```python
import jax, jax.numpy as jnp
from jax import lax
import numpy as np

D_MODEL = 2048
BATCH = 2
SEQ = 8192
DEPTH = 4

N_MIXERS = 2
RMS_EPS = 1e-6
POOL_WINDOWS = (2, 4, 8, 16)
N_POOL_GROUPS = len(POOL_WINDOWS)
POOL_GROUP_DIM = D_MODEL // N_POOL_GROUPS
RWKV_HEAD_DIM = 64
RWKV_HEADS = D_MODEL // RWKV_HEAD_DIM
N_SHIFT_MIX = 6
DECAY_LORA = max(32, int(round(1.8 * D_MODEL ** 0.5 / 32)) * 32)
AAA_LORA = max(32, int(round(1.8 * D_MODEL ** 0.5 / 32)) * 32)
MV_LORA = max(32, int(round(1.3 * D_MODEL ** 0.5 / 32)) * 32)
GATE_LORA = max(32, int(round(0.6 * D_MODEL ** 0.8 / 32)) * 32)
GN_EPS = 64e-5
D_FF = int(3.5 * D_MODEL)
N_EXPERTS = 8
TOP_K = 2
MOE_BLOCK = 128
N_POOL_LAYERS = (DEPTH + 1) // 2
N_RWKV_LAYERS = DEPTH // 2
N_VRES_LAYERS = max(DEPTH // 2 - 1, 0)
N_DENSE_LAYERS = (DEPTH + 1) // 2
N_MOE_LAYERS = DEPTH // 2

kernel_name = 'hybrid_pool_rwkv7_moe_trunk'


def rms_norm(x, g):
    xf = x.astype(jnp.float32)
    y = xf * lax.rsqrt(jnp.mean(xf * xf, axis=-1, keepdims=True) + RMS_EPS)
    return (y * g.astype(jnp.float32)).astype(x.dtype)


def pool_mixer(h, w_groups, scale):
    bsz, T, _ = h.shape
    hg = h.astype(jnp.float32).reshape(bsz, T, N_POOL_GROUPS, POOL_GROUP_DIM)
    cs = jnp.cumsum(hg, axis=1)
    pos = jnp.arange(1, T + 1, dtype=jnp.float32)
    pooled = []
    for g, win in enumerate(POOL_WINDOWS):
        c = cs[:, :, g]
        lagged = jnp.pad(c, ((0, 0), (win, 0), (0, 0)))[:, :T]
        cnt = jnp.minimum(pos, win)[None, :, None]
        pooled.append((c - lagged) / cnt)
    pooled = jnp.stack(pooled, axis=2)
    d = (pooled - hg).astype(h.dtype)
    mixed = jnp.einsum('btgc,gcd->btgd', d, w_groups).reshape(bsz, T, D_MODEL)
    return mixed * scale


def wkv7_recurrence(r, decay, k, v, kk, a):
    def step(S, inp):
        r_t, d_t, k_t, v_t, kk_t, a_t = inp
        sa = jnp.einsum('bhij,bhj->bhi', S, -kk_t)
        S = (S * d_t[:, :, None, :]
             + sa[..., None] * (kk_t * a_t)[:, :, None, :]
             + v_t[..., None] * k_t[:, :, None, :])
        y = jnp.einsum('bhij,bhj->bhi', S, r_t)
        return S, y
    xs = tuple(jnp.moveaxis(t, 1, 0) for t in (r, decay, k, v, kk, a))
    S0 = jnp.zeros((r.shape[0], RWKV_HEADS, RWKV_HEAD_DIM, RWKV_HEAD_DIM), jnp.float32)
    _, ys = lax.scan(step, S0, xs)
    return jnp.moveaxis(ys, 0, 1)


def rwkv7_mixer(h, v_first, v_res, mu, w0, w1, w2, a0, a1, a2, g1, g2, k_k, k_a, r_k,
                wr, wk, wv, wo, lnx_w, lnx_b):
    bsz, T, C = h.shape
    f32 = jnp.float32
    xx = jnp.pad(h, ((0, 0), (1, 0), (0, 0)))[:, :T] - h
    xr, xw, xk, xv, xa, xg = [h + xx * mu[m] for m in range(N_SHIFT_MIX)]
    r = (xr @ wr).astype(f32)
    k = (xk @ wk).astype(f32)
    v = (xv @ wv).astype(f32)
    w_log = -jax.nn.softplus(-(w0.astype(f32) + (jnp.tanh(xw @ w1) @ w2).astype(f32))) - 0.5
    if v_res is not None:
        v0, v1, v2 = v_res
        v = v + (v_first - v) * jax.nn.sigmoid(v0.astype(f32) + ((xv @ v1) @ v2).astype(f32))
    a = jax.nn.sigmoid(a0.astype(f32) + ((xa @ a1) @ a2).astype(f32))
    g = (jax.nn.sigmoid(xg @ g1) @ g2).astype(f32)

    def heads(t):
        return t.reshape(bsz, T, RWKV_HEADS, RWKV_HEAD_DIM)
    kk = heads(k * k_k.astype(f32))
    kk = kk / jnp.maximum(jnp.sqrt(jnp.sum(kk * kk, axis=-1, keepdims=True)), 1e-12)
    k = k * (1.0 + (a - 1.0) * k_a.astype(f32))
    rh, kh, vh = heads(r), heads(k), heads(v)
    decay = jnp.exp(-jnp.exp(heads(w_log)))
    y = wkv7_recurrence(rh, decay, kh, vh, kk, heads(a))
    mean = jnp.mean(y, axis=-1, keepdims=True)
    var = jnp.mean((y - mean) ** 2, axis=-1, keepdims=True)
    y = ((y - mean) * lax.rsqrt(var + GN_EPS)).reshape(bsz, T, C) * lnx_w.astype(f32) + lnx_b.astype(f32)
    bonus = jnp.sum(rh * kh * r_k.astype(f32), axis=-1, keepdims=True) * vh
    y = y + bonus.reshape(bsz, T, C)
    out = (y * g).astype(h.dtype) @ wo
    return out, v


def swiglu(h, w_gate, w_up, w_down):
    return (jax.nn.silu(h @ w_gate) * (h @ w_up)) @ w_down


def moe_swiglu(h, router, w_gate, w_up, w_down):
    bsz, T, C = h.shape
    hf = h.reshape(-1, C)
    n_tok = hf.shape[0]
    n_asg = n_tok * TOP_K
    logits = (hf @ router).astype(jnp.float32)
    top_val, top_idx = lax.top_k(logits, TOP_K)
    gates = jax.nn.softmax(top_val, axis=-1)
    exp_flat = top_idx.reshape(-1).astype(jnp.int32)
    tok_flat = jnp.repeat(jnp.arange(n_tok, dtype=jnp.int32), TOP_K)
    gate_flat = gates.reshape(-1)
    order = jnp.argsort(exp_flat * n_asg + jnp.arange(n_asg, dtype=jnp.int32))
    e_sorted = exp_flat[order]
    t_sorted = tok_flat[order]
    g_sorted = gate_flat[order]
    counts = jnp.zeros((N_EXPERTS,), jnp.int32).at[exp_flat].add(1)
    start = jnp.cumsum(counts) - counts
    padded = (counts + MOE_BLOCK - 1) // MOE_BLOCK * MOE_BLOCK
    pad_end = jnp.cumsum(padded)
    pad_start = pad_end - padded
    slot = pad_start[e_sorted] + (jnp.arange(n_asg, dtype=jnp.int32) - start[e_sorted])
    n_blocks = -(-n_asg // MOE_BLOCK) + N_EXPERTS
    n_slots = n_blocks * MOE_BLOCK
    slot_tok = jnp.zeros((n_slots,), jnp.int32).at[slot].set(t_sorted)
    slot_gate = jnp.zeros((n_slots,), jnp.float32).at[slot].set(g_sorted)
    block_start = jnp.arange(n_blocks, dtype=jnp.int32) * MOE_BLOCK
    block_expert = jnp.minimum(jnp.searchsorted(pad_end, block_start, side='right'), N_EXPERTS - 1)
    xb = hf[slot_tok].reshape(n_blocks, MOE_BLOCK, C)

    def expert_block(args):
        xblk, e = args
        return (jax.nn.silu(xblk @ w_gate[e]) * (xblk @ w_up[e])) @ w_down[e]
    yb = lax.map(expert_block, (xb, block_expert))
    y = yb.reshape(n_slots, C) * slot_gate[:, None].astype(h.dtype)
    out = jax.ops.segment_sum(y, slot_tok, num_segments=n_tok)
    return out.reshape(bsz, T, C)


def setup_inputs(seed: int = 0) -> dict:
    key = jax.random.key(seed)
    keys = iter(jax.random.split(key, 48))

    def nrm(shape, scale):
        return scale * jax.random.normal(next(keys), shape, jnp.float32)

    def unif(shape, lo, hi):
        return jax.random.uniform(next(keys), shape, jnp.float32, lo, hi)
    C, F = D_MODEL, D_FF
    NP, NR, NV, ND, NM = N_POOL_LAYERS, N_RWKV_LAYERS, N_VRES_LAYERS, N_DENSE_LAYERS, N_MOE_LAYERS
    return {
        'x': nrm((BATCH, SEQ, C), 1.0),
        'norm_mix': 1.0 + nrm((DEPTH, C), 0.02),
        'norm_ffn': 1.0 + nrm((DEPTH, C), 0.02),
        'norm_final': 1.0 + nrm((C,), 0.02),
        'pool_w': nrm((NP, N_POOL_GROUPS, POOL_GROUP_DIM, POOL_GROUP_DIM), POOL_GROUP_DIM ** -0.5),
        'pool_scale': 1.0 + nrm((NP, C), 0.02),
        'rwkv_mu': unif((NR, N_SHIFT_MIX, C), 0.0, 1.0),
        'rwkv_w0': unif((NR, C), -6.0, -1.0),
        'rwkv_w1': nrm((NR, C, DECAY_LORA), C ** -0.5),
        'rwkv_w2': nrm((NR, DECAY_LORA, C), 0.1 * DECAY_LORA ** -0.5),
        'rwkv_a0': nrm((NR, C), 0.1),
        'rwkv_a1': nrm((NR, C, AAA_LORA), C ** -0.5),
        'rwkv_a2': nrm((NR, AAA_LORA, C), AAA_LORA ** -0.5),
        'rwkv_v0': 1.0 + nrm((NV, C), 0.02),
        'rwkv_v1': nrm((NV, C, MV_LORA), C ** -0.5),
        'rwkv_v2': nrm((NV, MV_LORA, C), MV_LORA ** -0.5),
        'rwkv_g1': nrm((NR, C, GATE_LORA), C ** -0.5),
        'rwkv_g2': nrm((NR, GATE_LORA, C), GATE_LORA ** -0.5),
        'rwkv_k_k': 0.85 + nrm((NR, C), 0.02),
        'rwkv_k_a': 1.0 + nrm((NR, C), 0.02),
        'rwkv_r_k': nrm((NR, RWKV_HEADS, RWKV_HEAD_DIM), 0.1),
        'rwkv_wr': nrm((NR, C, C), C ** -0.5),
        'rwkv_wk': nrm((NR, C, C), C ** -0.5),
        'rwkv_wv': nrm((NR, C, C), C ** -0.5),
        'rwkv_wo': nrm((NR, C, C), C ** -0.5),
        'rwkv_lnx_w': 1.0 + nrm((NR, C), 0.02),
        'rwkv_lnx_b': nrm((NR, C), 0.01),
        'ffn_w_gate': nrm((ND, C, F), C ** -0.5),
        'ffn_w_up': nrm((ND, C, F), C ** -0.5),
        'ffn_w_down': nrm((ND, F, C), F ** -0.5),
        'moe_router': nrm((NM, C, N_EXPERTS), C ** -0.5),
        'moe_w_gate': nrm((NM, N_EXPERTS, C, F), C ** -0.5),
        'moe_w_up': nrm((NM, N_EXPERTS, C, F), C ** -0.5),
        'moe_w_down': nrm((NM, N_EXPERTS, F, C), F ** -0.5),
    }


def reference(x, norm_mix, norm_ffn, norm_final, pool_w, pool_scale,
              rwkv_mu, rwkv_w0, rwkv_w1, rwkv_w2, rwkv_a0, rwkv_a1, rwkv_a2,
              rwkv_v0, rwkv_v1, rwkv_v2, rwkv_g1, rwkv_g2, rwkv_k_k, rwkv_k_a, rwkv_r_k,
              rwkv_wr, rwkv_wk, rwkv_wv, rwkv_wo, rwkv_lnx_w, rwkv_lnx_b,
              ffn_w_gate, ffn_w_up, ffn_w_down,
              moe_router, moe_w_gate, moe_w_up, moe_w_down):
    h = x
    v_first = None
    for i in range(DEPTH):
        j = i // N_MIXERS
        hn = rms_norm(h, norm_mix[i])
        if i % N_MIXERS == 0:
            h = h + pool_mixer(hn, pool_w[j], pool_scale[j])
        else:
            v_res = None if j == 0 else (rwkv_v0[j - 1], rwkv_v1[j - 1], rwkv_v2[j - 1])
            out, v = rwkv7_mixer(hn, v_first, v_res, rwkv_mu[j], rwkv_w0[j], rwkv_w1[j], rwkv_w2[j],
                                 rwkv_a0[j], rwkv_a1[j], rwkv_a2[j], rwkv_g1[j], rwkv_g2[j],
                                 rwkv_k_k[j], rwkv_k_a[j], rwkv_r_k[j],
                                 rwkv_wr[j], rwkv_wk[j], rwkv_wv[j], rwkv_wo[j],
                                 rwkv_lnx_w[j], rwkv_lnx_b[j])
            if v_first is None:
                v_first = v
            h = h + out
        f = i // 2
        hn = rms_norm(h, norm_ffn[i])
        if i % 2 == 0:
            h = h + swiglu(hn, ffn_w_gate[f], ffn_w_up[f], ffn_w_down[f])
        else:
            h = h + moe_swiglu(hn, moe_router[f], moe_w_gate[f], moe_w_up[f], moe_w_down[f])
    return rms_norm(h, norm_final)
```

```python
import functools

import jax
import jax.numpy as jnp
from jax import lax
from jax.experimental import pallas as pl
from jax.experimental.pallas import tpu as pltpu

F32 = jnp.float32
BF16 = jnp.bfloat16
I32 = jnp.int32
U32 = jnp.uint32

LANES = 128
HEAD_DIM = 64
HEADS_PER_VREG = LANES // HEAD_DIM
RMS_EPS = 1e-6
GN_EPS = 64e-5
POOL_WINDOWS = (2, 4, 8, 16)
POOL_HALO = 16
N_EXPERTS = 8
TOP_K = 2
VMEM_LIMIT_BYTES = 56 * 1024 * 1024

ROW_TILE = 512
PROJ_ROWS = 256
COL_TILE = 512
FF_TILE = 512
MOE_ROWS = 512
ROUTE_ROWS = 256
WKV_CHUNK = 64
WKV_LANES = 256


def _params(*sem):
    return pltpu.CompilerParams(dimension_semantics=sem, vmem_limit_bytes=VMEM_LIMIT_BYTES)


def _rms_norm(x, g):
    return x * lax.rsqrt(jnp.mean(x * x, axis=-1, keepdims=True) + RMS_EPS) * g


def _sigmoid(x):
    return 1.0 / (1.0 + jnp.exp(-x))


def _head_ones():
    r = lax.broadcasted_iota(I32, (LANES, LANES), 0) // HEAD_DIM
    c = lax.broadcasted_iota(I32, (LANES, LANES), 1) // HEAD_DIM
    return (r == c).astype(BF16)


def _head_sum(x, ones):
    outs = []
    for c in range(x.shape[1] // LANES):
        xc = x[:, c * LANES:(c + 1) * LANES]
        hi = xc.astype(BF16)
        lo = (xc - hi.astype(F32)).astype(BF16)
        outs.append(jnp.dot(hi, ones, preferred_element_type=F32)
                    + jnp.dot(lo, ones, preferred_element_type=F32))
    return outs[0] if len(outs) == 1 else jnp.concatenate(outs, axis=1)


def _pool_body(x_ref, halo_ref, gmix_ref, w_ref, scale_ref, gffn_ref, h_ref, hn_ref, *,
               tiles_per_seq):
    tm = x_ref.shape[0]
    group = w_ref.shape[1]
    tile_in_seq = pl.program_id(0) % tiles_per_seq
    x = x_ref[...]
    g = gmix_ref[...]
    xn = _rms_norm(x, g)
    keep = (tile_in_seq > 0).astype(F32)
    cat = jnp.concatenate([_rms_norm(halo_ref[...], g) * keep, xn], axis=0)
    row = lax.broadcasted_iota(I32, (tm, 1), 0)
    pos = (tile_in_seq * tm + row + 1).astype(F32)
    for gi, win in enumerate(POOL_WINDOWS):
        lo = gi * group
        s = cat[:, lo:lo + group]
        k = 1
        while k < win:
            s = s + pltpu.roll(s, k, 0)
            k *= 2
        pooled = s[POOL_HALO:, :] / jnp.minimum(pos, float(win))
        d = (pooled - xn[:, lo:lo + group]).astype(BF16)
        mixed = jnp.dot(d, w_ref[gi], preferred_element_type=F32)
        h_ref[:, lo:lo + group] = x[:, lo:lo + group] + mixed * scale_ref[:, lo:lo + group]
    hn_ref[...] = _rms_norm(h_ref[...], gffn_ref[...]).astype(BF16)


def _pool_layer(h, seq, g_mix, w, scale, g_ffn):
    n, c = h.shape
    tm = ROW_TILE
    halo_blocks = tm // POOL_HALO
    return pl.pallas_call(
        functools.partial(_pool_body, tiles_per_seq=seq // tm),
        grid=(n // tm,),
        in_specs=[
            pl.BlockSpec((tm, c), lambda i: (i, 0)),
            pl.BlockSpec((POOL_HALO, c), lambda i: (jnp.maximum(i * halo_blocks - 1, 0), 0)),
            pl.BlockSpec((1, c), lambda i: (0, 0)),
            pl.BlockSpec(w.shape, lambda i: (0, 0, 0)),
            pl.BlockSpec((1, c), lambda i: (0, 0)),
            pl.BlockSpec((1, c), lambda i: (0, 0)),
        ],
        out_specs=[pl.BlockSpec((tm, c), lambda i: (i, 0)), pl.BlockSpec((tm, c), lambda i: (i, 0))],
        out_shape=[jax.ShapeDtypeStruct((n, c), F32), jax.ShapeDtypeStruct((n, c), BF16)],
        compiler_params=_params("parallel"),
        name="pool_mixer",
    )(h, h, g_mix, w, scale, g_ffn)


def _ffn_body(x_ref, wg_ref, wu_ref, wd_ref, h_ref, gnext_ref, out_ref, hn_ref):
    f = pl.program_id(1)

    @pl.when(f == 0)
    def _():
        out_ref[...] = h_ref[...]

    x = x_ref[...]
    gate = jnp.dot(x, wg_ref[...], preferred_element_type=F32)
    up = jnp.dot(x, wu_ref[...], preferred_element_type=F32)
    act = (gate * _sigmoid(gate) * up).astype(BF16)
    out_ref[...] += jnp.dot(act, wd_ref[...], preferred_element_type=F32)

    @pl.when(f == pl.num_programs(1) - 1)
    def _():
        hn_ref[...] = _rms_norm(out_ref[...], gnext_ref[...])


def _ffn_layer(hn, h, wg, wu, wd, g_next):
    n, c = h.shape
    ff = wg.shape[1]
    tm, tf = ROW_TILE, FF_TILE
    return pl.pallas_call(
        _ffn_body,
        grid=(n // tm, ff // tf),
        in_specs=[
            pl.BlockSpec((tm, c), lambda i, f: (i, 0)),
            pl.BlockSpec((c, tf), lambda i, f: (0, f)),
            pl.BlockSpec((c, tf), lambda i, f: (0, f)),
            pl.BlockSpec((tf, c), lambda i, f: (f, 0)),
            pl.BlockSpec((tm, c), lambda i, f: (i, 0)),
            pl.BlockSpec((1, c), lambda i, f: (0, 0)),
        ],
        out_specs=[pl.BlockSpec((tm, c), lambda i, f: (i, 0)),
                   pl.BlockSpec((tm, c), lambda i, f: (i, 0))],
        out_shape=[jax.ShapeDtypeStruct((n, c), F32), jax.ShapeDtypeStruct((n, c), F32)],
        compiler_params=_params("parallel", "arbitrary"),
        name="dense_swiglu",
    )(hn, wg, wu, wd, h, g_next)


def _rwkv_proj_body(*refs, tiles_per_seq, has_vres):
    if has_vres:
        (hn_ref, halo_ref, mu_ref, wr_ref, wk_ref, wv_ref, w1_ref, a1_ref, g1_ref,
         w2_ref, a2_ref, g2_ref, w0_ref, a0_ref, kk_ref, ka_ref,
         v1_ref, v2_ref, v0_ref, vf_ref,
         r_out, d_out, k_out, v_out, na_out, b_out, g_out,
         xs_ref, hw_ref, ha_ref, hg_ref, hv_ref) = refs
    else:
        (hn_ref, halo_ref, mu_ref, wr_ref, wk_ref, wv_ref, w1_ref, a1_ref, g1_ref,
         w2_ref, a2_ref, g2_ref, w0_ref, a0_ref, kk_ref, ka_ref,
         r_out, d_out, k_out, v_out, na_out, b_out, g_out,
         xs_ref, hw_ref, ha_ref, hg_ref) = refs
    tm = hn_ref.shape[0]

    @pl.when(pl.program_id(1) == 0)
    def _():
        hn = hn_ref[...]
        keep = (pl.program_id(0) % tiles_per_seq > 0).astype(F32)
        before = halo_ref[halo_ref.shape[0] - 1:, :] * keep
        row = lax.broadcasted_iota(I32, (tm, 1), 0)
        prev = jnp.where(row == 0, before, pltpu.roll(hn, 1, 0))
        xx = prev - hn
        for m in range(6):
            xs_ref[m] = (hn + xx * mu_ref[m:m + 1, :]).astype(BF16)
        hw_ref[...] = jnp.tanh(jnp.dot(xs_ref[1], w1_ref[...], preferred_element_type=F32)).astype(BF16)
        ha_ref[...] = jnp.dot(xs_ref[4], a1_ref[...], preferred_element_type=F32).astype(BF16)
        hg_ref[...] = _sigmoid(jnp.dot(xs_ref[5], g1_ref[...], preferred_element_type=F32)).astype(BF16)
        if has_vres:
            hv_ref[...] = jnp.dot(xs_ref[3], v1_ref[...], preferred_element_type=F32).astype(BF16)

    r = jnp.dot(xs_ref[0], wr_ref[...], preferred_element_type=F32)
    k = jnp.dot(xs_ref[2], wk_ref[...], preferred_element_type=F32)
    v = jnp.dot(xs_ref[3], wv_ref[...], preferred_element_type=F32)
    wl = jnp.dot(hw_ref[...], w2_ref[...], preferred_element_type=F32)
    al = jnp.dot(ha_ref[...], a2_ref[...], preferred_element_type=F32)
    gl = jnp.dot(hg_ref[...], g2_ref[...], preferred_element_type=F32)
    z = -(w0_ref[...] + wl)
    softplus = jnp.maximum(z, 0.0) + jnp.log(1.0 + jnp.exp(-jnp.abs(z)))
    w_log = -softplus - 0.5
    a = _sigmoid(a0_ref[...] + al)
    if has_vres:
        vl = jnp.dot(hv_ref[...], v2_ref[...], preferred_element_type=F32)
        v = v + (vf_ref[...] - v) * _sigmoid(v0_ref[...] + vl)
    kk = k * kk_ref[...]
    norm = jnp.sqrt(_head_sum(kk * kk, _head_ones()))
    kk = kk / jnp.maximum(norm, 1e-12)
    r_out[...] = r
    d_out[...] = jnp.exp(-jnp.exp(w_log))
    k_out[...] = k * (1.0 + (a - 1.0) * ka_ref[...])
    v_out[...] = v
    na_out[...] = -kk
    b_out[...] = kk * a
    g_out[...] = gl


def _rwkv_proj(hn, seq, p, v_first):
    n, c = hn.shape
    tm, tn = PROJ_ROWS, COL_TILE
    has_vres = v_first is not None
    halo_rows = 8
    row_spec = pl.BlockSpec((tm, c), lambda i, j: (i, 0))
    halo_spec = pl.BlockSpec((halo_rows, c),
                             lambda i, j: (jnp.maximum(i * (tm // halo_rows) - 1, 0), 0))
    full = lambda a: pl.BlockSpec(a.shape, lambda i, j: (0,) * a.ndim)
    cols = lambda a: pl.BlockSpec((a.shape[0], tn), lambda i, j: (0, j))
    tile = pl.BlockSpec((tm, tn), lambda i, j: (i, j))
    args = [hn, hn, p["mu"], p["wr"], p["wk"], p["wv"], p["w1"], p["a1"], p["g1"],
            p["w2"], p["a2"], p["g2"], p["w0"], p["a0"], p["k_k"], p["k_a"]]
    specs = [row_spec, halo_spec, full(p["mu"]), cols(p["wr"]), cols(p["wk"]), cols(p["wv"]),
             full(p["w1"]), full(p["a1"]), full(p["g1"]),
             cols(p["w2"]), cols(p["a2"]), cols(p["g2"]),
             cols(p["w0"]), cols(p["a0"]), cols(p["k_k"]), cols(p["k_a"])]
    scratch = [pltpu.VMEM((6, tm, c), BF16), pltpu.VMEM((tm, p["w1"].shape[1]), BF16),
               pltpu.VMEM((tm, p["a1"].shape[1]), BF16), pltpu.VMEM((tm, p["g1"].shape[1]), BF16)]
    if has_vres:
        args += [p["v1"], p["v2"], p["v0"], v_first]
        specs += [full(p["v1"]), cols(p["v2"]), cols(p["v0"]), tile]
        scratch.append(pltpu.VMEM((tm, p["v1"].shape[1]), BF16))
    return pl.pallas_call(
        functools.partial(_rwkv_proj_body, tiles_per_seq=seq // tm, has_vres=has_vres),
        grid=(n // tm, c // tn),
        in_specs=specs,
        out_specs=[tile] * 7,
        out_shape=[jax.ShapeDtypeStruct((n, c), F32)] * 7,
        scratch_shapes=scratch,
        compiler_params=_params("parallel", "arbitrary"),
        name="rwkv_proj_vres" if has_vres else "rwkv_proj",
    )(*args)


def _wkv_body(r_ref, d_ref, k_ref, v_ref, na_ref, b_ref, y_ref, s_ref, lhs_ref, lhy_ref):
    nb, tc, c = r_ref.shape
    width = WKV_LANES
    n_groups = c // width
    rows = n_groups * HEAD_DIM
    sub = 8

    @pl.when(pl.program_id(0) == 0)
    def _():
        s_ref[...] = jnp.zeros_like(s_ref)

    head_r = lax.broadcasted_iota(I32, (width, width), 0) // HEAD_DIM
    head_c = lax.broadcasted_iota(I32, (width, width), 1) // HEAD_DIM
    ones = (head_r == head_c).astype(BF16)
    sel = (lax.broadcasted_iota(I32, (HEAD_DIM, width), 1) % HEAD_DIM
           == lax.broadcasted_iota(I32, (HEAD_DIM, width), 0)).astype(F32)

    def tile_steps(t8, carry):
        t0 = pl.multiple_of(t8 * sub, sub)
        y_rows = [[[] for _ in range(n_groups)] for _ in range(nb)]
        for s in range(sub):
            for bb in range(nb):
                def row(ref, q):
                    return ref[bb, pl.ds(t0, sub), q * width:(q + 1) * width][s:s + 1, :]

                for q in range(n_groups):
                    blk = slice(q * HEAD_DIM, (q + 1) * HEAD_DIM)
                    lhs_ref[bb, blk, :] = (s_ref[bb, blk, :] * row(na_ref, q)).astype(BF16)
                    lhs_ref[bb, rows + q * HEAD_DIM:rows + (q + 1) * HEAD_DIM, :] = (
                        sel * row(v_ref, q)).astype(BF16)
                red = jnp.dot(lhs_ref[bb], ones, preferred_element_type=F32)
                for q in range(n_groups):
                    blk = slice(q * HEAD_DIM, (q + 1) * HEAD_DIM)
                    state = (s_ref[bb, blk, :] * row(d_ref, q) + red[blk, :] * row(b_ref, q)
                             + red[rows + q * HEAD_DIM:rows + (q + 1) * HEAD_DIM, :] * row(k_ref, q))
                    s_ref[bb, blk, :] = state
                    lhy_ref[bb, blk, :] = (state * row(r_ref, q)).astype(BF16)
                ycol = jnp.dot(lhy_ref[bb], ones, preferred_element_type=F32)
                for q in range(n_groups):
                    blk = slice(q * HEAD_DIM, (q + 1) * HEAD_DIM)
                    y_rows[bb][q].append(jnp.sum(ycol[blk, :] * sel, axis=0, keepdims=True))
        for bb in range(nb):
            for q in range(n_groups):
                y_ref[bb, pl.ds(t0, sub), q * width:(q + 1) * width] = jnp.concatenate(
                    y_rows[bb][q], axis=0)
        return carry

    lax.fori_loop(0, tc // sub, tile_steps, 0)


def _wkv(r, d, k, v, na, b, batch):
    n, c = r.shape
    seq = n // batch
    tc = WKV_CHUNK
    rows = c // WKV_LANES * HEAD_DIM
    shaped = [a.reshape(batch, seq, c) for a in (r, d, k, v, na, b)]
    spec = pl.BlockSpec((batch, tc, c), lambda i: (0, i, 0))
    y = pl.pallas_call(
        _wkv_body,
        grid=(seq // tc,),
        in_specs=[spec] * 6,
        out_specs=spec,
        out_shape=jax.ShapeDtypeStruct((batch, seq, c), F32),
        scratch_shapes=[pltpu.VMEM((batch, rows, WKV_LANES), F32),
                        pltpu.VMEM((batch, 2 * rows, WKV_LANES), BF16),
                        pltpu.VMEM((batch, rows, WKV_LANES), BF16)],
        compiler_params=_params("arbitrary"),
        name="wkv7_scan",
    )(*shaped)
    return y.reshape(n, c)


def _rwkv_out_body(y_ref, r_ref, k_ref, v_ref, g_ref, h_ref, lnw_ref, lnb_ref, rk_ref, wo_ref,
                   gffn_ref, router_ref, out_ref, hnp_ref, route_ref, z_ref):
    j = pl.program_id(1)
    tn = wo_ref.shape[1]

    @pl.when(j == 0)
    def _():
        ones = _head_ones()
        y = y_ref[...]
        mean = _head_sum(y, ones) * (1.0 / HEAD_DIM)
        yc = y - mean
        var = _head_sum(yc * yc, ones) * (1.0 / HEAD_DIM)
        yn = yc * lax.rsqrt(var + GN_EPS) * lnw_ref[...] + lnb_ref[...]
        bonus = _head_sum(r_ref[...] * k_ref[...] * rk_ref[...], ones) * v_ref[...]
        z_ref[...] = ((yn + bonus) * g_ref[...]).astype(BF16)

    col = pl.multiple_of(j * tn, tn)
    out_ref[:, pl.ds(col, tn)] = h_ref[:, pl.ds(col, tn)] + jnp.dot(
        z_ref[...], wo_ref[...], preferred_element_type=F32)

    @pl.when(j == pl.num_programs(1) - 1)
    def _():
        hn = _rms_norm(out_ref[...], gffn_ref[...])
        half = hn.shape[1] // 2
        lo_bits = lax.bitcast_convert_type(hn[:, :half].astype(BF16).astype(F32), U32) >> 16
        hi_bits = lax.bitcast_convert_type(hn[:, half:].astype(BF16).astype(F32), U32)
        hnp_ref[...] = (hi_bits & jnp.uint32(0xFFFF0000)) | lo_bits
        logits = jnp.dot(hn, router_ref[...], preferred_element_type=F32,
                         precision=lax.Precision.HIGHEST)
        lane = lax.broadcasted_iota(I32, logits.shape, 1)
        neg = jnp.float32(-jnp.inf)
        lg = jnp.where(lane < N_EXPERTS, logits, neg)
        m1 = jnp.max(lg, axis=1, keepdims=True)
        i1 = jnp.min(jnp.where(lg == m1, lane, LANES), axis=1, keepdims=True)
        lg2 = jnp.where(lane == i1, neg, lg)
        m2 = jnp.max(lg2, axis=1, keepdims=True)
        i2 = jnp.min(jnp.where(lg2 == m2, lane, LANES), axis=1, keepdims=True)
        e = jnp.exp(m2 - m1)
        g1 = 1.0 / (1.0 + e)
        g2 = e / (1.0 + e)
        route_ref[...] = jnp.where(
            lane == 0, i1.astype(F32),
            jnp.where(lane == 1, i2.astype(F32),
                      jnp.where(lane == 2, g1, jnp.where(lane == 3, g2, 0.0))))


def _rwkv_out(y, r, k, v, g, h, lnw, lnb, rk, wo, g_ffn, router):
    n, c = h.shape
    tm, tn = PROJ_ROWS, COL_TILE
    row = pl.BlockSpec((tm, c), lambda i, j: (i, 0))
    vec = pl.BlockSpec((1, c), lambda i, j: (0, 0))
    return pl.pallas_call(
        _rwkv_out_body,
        grid=(n // tm, c // tn),
        in_specs=[row, row, row, row, row, row, vec, vec, vec,
                  pl.BlockSpec((c, tn), lambda i, j: (0, j)), vec,
                  pl.BlockSpec(router.shape, lambda i, j: (0, 0))],
        out_specs=[row, pl.BlockSpec((tm, c // 2), lambda i, j: (i, 0)),
                   pl.BlockSpec((tm, LANES), lambda i, j: (i, 0))],
        out_shape=[jax.ShapeDtypeStruct((n, c), F32), jax.ShapeDtypeStruct((n, c // 2), U32),
                   jax.ShapeDtypeStruct((n, LANES), F32)],
        scratch_shapes=[pltpu.VMEM((tm, c), BF16)],
        compiler_params=_params("parallel", "arbitrary"),
        name="rwkv_out_router",
    )(y, r, k, v, g, h, lnw, lnb, rk, wo, g_ffn, router)


def _route_slots(route, rows_per_block):
    n = route.shape[0]
    experts = route[:, :TOP_K].astype(I32)
    flat = experts.reshape(-1)
    onehot = (flat[None, :] == jnp.arange(N_EXPERTS, dtype=I32)[:, None]).astype(I32)
    csum = jnp.cumsum(onehot, axis=1)
    counts = csum[:, -1]
    padded = (counts + rows_per_block - 1) // rows_per_block * rows_per_block
    pad_end = jnp.cumsum(padded)
    pad_start = pad_end - padded
    slot = jnp.sum(onehot * (csum - 1 + pad_start[:, None]), axis=0).reshape(n, TOP_K)
    n_blocks = (n * TOP_K) // rows_per_block + N_EXPERTS
    blk_start = jnp.arange(n_blocks, dtype=I32) * rows_per_block
    blk_expert = jnp.minimum(jnp.sum((blk_start[:, None] >= pad_end[None, :]).astype(I32), axis=1),
                             N_EXPERTS - 1)
    n_used = pad_end[-1] // rows_per_block
    return slot, blk_expert.astype(I32), n_used.astype(I32).reshape(1), n_blocks


def _dispatch_body(s0_ref, s1_ref, x_ref, xs_in_ref, xs_ref, sem):
    del xs_in_ref
    rows = x_ref.shape[0]

    def copy(r, slot):
        return pltpu.make_async_copy(x_ref.at[pl.ds(r, 1)], xs_ref.at[pl.ds(slot, 1)], sem)

    def issue(r, carry):
        copy(r, s0_ref[0, 0, r]).start()
        copy(r, s1_ref[0, 0, r]).start()
        return carry

    lax.fori_loop(0, rows, issue, 0)

    def drain(r, carry):
        copy(0, 0).wait()
        copy(0, 0).wait()
        return carry

    lax.fori_loop(0, rows, drain, 0)


def _dispatch(xp, slot, n_slots):
    n, w = xp.shape
    tr = ROUTE_ROWS
    s0 = slot[:, 0].reshape(n // tr, 1, tr)
    s1 = slot[:, 1].reshape(n // tr, 1, tr)
    smem = pl.BlockSpec((1, 1, tr), lambda i: (i, 0, 0), memory_space=pltpu.SMEM)
    return pl.pallas_call(
        _dispatch_body,
        grid=(n // tr,),
        in_specs=[smem, smem, pl.BlockSpec((tr, w), lambda i: (i, 0)),
                  pl.BlockSpec(memory_space=pl.ANY)],
        out_specs=pl.BlockSpec(memory_space=pl.ANY),
        out_shape=jax.ShapeDtypeStruct((n_slots, w), xp.dtype),
        scratch_shapes=[pltpu.SemaphoreType.DMA(())],
        input_output_aliases={3: 0},
        compiler_params=_params("arbitrary"),
        name="moe_dispatch",
    )(s0, s1, xp, jnp.zeros((n_slots, w), xp.dtype))


def _moe_body(be_ref, nu_ref, x_ref, wg_ref, wu_ref, wd_ref, y_ref, xb_ref):
    del be_ref
    b = pl.program_id(0)
    f = pl.program_id(1)

    @pl.when(f == 0)
    def _():
        y_ref[...] = jnp.zeros_like(y_ref)
        packed = x_ref[...]
        half = packed.shape[1]
        xb_ref[:, :half] = lax.bitcast_convert_type(packed << 16, F32).astype(BF16)
        xb_ref[:, half:] = lax.bitcast_convert_type(
            packed & jnp.uint32(0xFFFF0000), F32).astype(BF16)

    @pl.when(b < nu_ref[0])
    def _():
        x = xb_ref[...]
        gate = jnp.dot(x, wg_ref[...], preferred_element_type=F32)
        up = jnp.dot(x, wu_ref[...], preferred_element_type=F32)
        act = (gate * _sigmoid(gate) * up).astype(BF16)
        y_ref[...] += jnp.dot(act, wd_ref[...], preferred_element_type=F32)


def _moe_experts(xs, blk_expert, n_used, wg, wu, wd, n_blocks):
    n_slots, half = xs.shape
    c = 2 * half
    ff = wg.shape[2]
    tb, tf = MOE_ROWS, FF_TILE
    nf = ff // tf

    def used(b, nu):
        return jnp.minimum(b, nu[0] - 1)

    def f_of(b, f, nu):
        return jnp.where(b < nu[0], f, nf - 1)

    grid_spec = pltpu.PrefetchScalarGridSpec(
        num_scalar_prefetch=2,
        grid=(n_blocks, nf),
        in_specs=[
            pl.BlockSpec((tb, half), lambda b, f, be, nu: (used(b, nu), 0)),
            pl.BlockSpec((None, c, tf), lambda b, f, be, nu: (be[used(b, nu)], 0, f_of(b, f, nu))),
            pl.BlockSpec((None, c, tf), lambda b, f, be, nu: (be[used(b, nu)], 0, f_of(b, f, nu))),
            pl.BlockSpec((None, tf, c), lambda b, f, be, nu: (be[used(b, nu)], f_of(b, f, nu), 0)),
        ],
        out_specs=pl.BlockSpec((tb, c), lambda b, f, be, nu: (b, 0)),
        scratch_shapes=[pltpu.VMEM((tb, c), BF16)],
    )
    return pl.pallas_call(
        _moe_body,
        grid_spec=grid_spec,
        out_shape=jax.ShapeDtypeStruct((n_slots, c), F32),
        compiler_params=_params("arbitrary", "arbitrary"),
        name="moe_experts",
    )(blk_expert, n_used, xs, wg, wu, wd)


def _combine_body(s0_ref, s1_ref, route_ref, h_ref, g_ref, ys_ref, out_ref, buf_ref, sem, *,
                  final_norm):
    rows = h_ref.shape[0]

    def copy(r, slot, k):
        return pltpu.make_async_copy(ys_ref.at[pl.ds(slot, 1)], buf_ref.at[k, pl.ds(r, 1)], sem)

    def issue(r, carry):
        copy(r, s0_ref[0, 0, r], 0).start()
        copy(r, s1_ref[0, 0, r], 1).start()
        return carry

    lax.fori_loop(0, rows, issue, 0)

    def drain(r, carry):
        copy(0, 0, 0).wait()
        copy(0, 0, 1).wait()
        return carry

    lax.fori_loop(0, rows, drain, 0)
    route = route_ref[...]
    out = h_ref[...] + buf_ref[0] * route[:, 2:3] + buf_ref[1] * route[:, 3:4]
    if final_norm:
        out = _rms_norm(out, g_ref[...])
    out_ref[...] = out


def _combine(ys, slot, route, h, g_final, final_norm):
    n, c = h.shape
    tr = ROUTE_ROWS
    s0 = slot[:, 0].reshape(n // tr, 1, tr)
    s1 = slot[:, 1].reshape(n // tr, 1, tr)
    smem = pl.BlockSpec((1, 1, tr), lambda i: (i, 0, 0), memory_space=pltpu.SMEM)
    return pl.pallas_call(
        functools.partial(_combine_body, final_norm=final_norm),
        grid=(n // tr,),
        in_specs=[smem, smem, pl.BlockSpec((tr, LANES), lambda i: (i, 0)),
                  pl.BlockSpec((tr, c), lambda i: (i, 0)), pl.BlockSpec((1, c), lambda i: (0, 0)),
                  pl.BlockSpec(memory_space=pl.ANY)],
        out_specs=pl.BlockSpec((tr, c), lambda i: (i, 0)),
        out_shape=jax.ShapeDtypeStruct((n, c), F32),
        scratch_shapes=[pltpu.VMEM((TOP_K, tr, c), F32), pltpu.SemaphoreType.DMA(())],
        compiler_params=_params("arbitrary"),
        name="moe_combine_norm" if final_norm else "moe_combine",
    )(s0, s1, route, h, g_final, ys)


def _pad_cols(w, mult=LANES):
    pad = -w.shape[1] % mult
    return jnp.pad(w, ((0, 0), (0, pad))) if pad else w


def _pad_rows(w, mult=LANES):
    pad = -w.shape[0] % mult
    return jnp.pad(w, ((0, pad), (0, 0))) if pad else w


def kernel(x, norm_mix, norm_ffn, norm_final, pool_w, pool_scale, rwkv_mu, rwkv_w0, rwkv_w1, rwkv_w2, rwkv_a0, rwkv_a1, rwkv_a2, rwkv_v0, rwkv_v1, rwkv_v2, rwkv_g1, rwkv_g2, rwkv_k_k, rwkv_k_a, rwkv_r_k, rwkv_wr, rwkv_wk, rwkv_wv, rwkv_wo, rwkv_lnx_w, rwkv_lnx_b, ffn_w_gate, ffn_w_up, ffn_w_down, moe_router, moe_w_gate, moe_w_up, moe_w_down):
    batch, seq, c = x.shape
    n = batch * seq
    depth = norm_mix.shape[0]
    vec = lambda a: a.reshape(1, c)
    h = x.reshape(n, c)
    hn = None
    v_first = None
    for i in range(depth):
        j = i // 2
        if i % 2 == 0:
            h, hn_ffn = _pool_layer(h, seq, vec(norm_mix[i]), pool_w[j].astype(BF16),
                                    vec(pool_scale[j]), vec(norm_ffn[i]))
            h, hn = _ffn_layer(hn_ffn, h, ffn_w_gate[j].astype(BF16), ffn_w_up[j].astype(BF16),
                               ffn_w_down[j].astype(BF16), vec(norm_mix[i + 1]))
        else:
            p = {
                "mu": jnp.pad(rwkv_mu[j], ((0, 2), (0, 0))),
                "wr": rwkv_wr[j].astype(BF16), "wk": rwkv_wk[j].astype(BF16),
                "wv": rwkv_wv[j].astype(BF16),
                "w1": _pad_cols(rwkv_w1[j]).astype(BF16), "w2": _pad_rows(rwkv_w2[j]).astype(BF16),
                "a1": _pad_cols(rwkv_a1[j]).astype(BF16), "a2": _pad_rows(rwkv_a2[j]).astype(BF16),
                "g1": _pad_cols(rwkv_g1[j]).astype(BF16), "g2": _pad_rows(rwkv_g2[j]).astype(BF16),
                "w0": vec(rwkv_w0[j]), "a0": vec(rwkv_a0[j]),
                "k_k": vec(rwkv_k_k[j]), "k_a": vec(rwkv_k_a[j]),
            }
            if j > 0:
                p["v1"] = _pad_cols(rwkv_v1[j - 1]).astype(BF16)
                p["v2"] = _pad_rows(rwkv_v2[j - 1]).astype(BF16)
                p["v0"] = vec(rwkv_v0[j - 1])
            r, d, k, v, na, b, g = _rwkv_proj(hn, seq, p, v_first if j > 0 else None)
            if v_first is None:
                v_first = v
            y = _wkv(r, d, k, v, na, b, batch)
            h, hn_packed, route = _rwkv_out(
                y, r, k, v, g, h, vec(rwkv_lnx_w[j]), vec(rwkv_lnx_b[j]), vec(rwkv_r_k[j]),
                rwkv_wo[j].astype(BF16), vec(norm_ffn[i]), _pad_cols(moe_router[j]))
            slot, blk_expert, n_used, n_blocks = _route_slots(route, MOE_ROWS)
            xs = _dispatch(hn_packed, slot, n_blocks * MOE_ROWS)
            ys = _moe_experts(xs, blk_expert, n_used, moe_w_gate[j].astype(BF16),
                              moe_w_up[j].astype(BF16), moe_w_down[j].astype(BF16), n_blocks)
            h = _combine(ys, slot, route, h, vec(norm_final), final_norm=(i == depth - 1))
    return h.reshape(batch, seq, c)
```

```python
import functools

import jax
import jax.numpy as jnp
from jax import lax
from jax.experimental import pallas as pl
from jax.experimental.pallas import tpu as pltpu

F32 = jnp.float32
BF16 = jnp.bfloat16
I32 = jnp.int32
U32 = jnp.uint32

LANES = 128
HEAD_DIM = 64
HEADS_PER_VREG = LANES // HEAD_DIM
RMS_EPS = 1e-6
GN_EPS = 64e-5
POOL_WINDOWS = (2, 4, 8, 16)
POOL_HALO = 16
N_EXPERTS = 8
TOP_K = 2
VMEM_LIMIT_BYTES = 56 * 1024 * 1024

ROW_TILE = 512
PROJ_ROWS = 256
COL_TILE = 512
FF_TILE = 512
MOE_ROWS = 512
ROUTE_ROWS = 256
CAST_BLOCK_BYTES = 8 * 1024 * 1024
WKV_CHUNK = 64
WKV_LANES = 256


def _params(*sem):
    return pltpu.CompilerParams(dimension_semantics=sem, vmem_limit_bytes=VMEM_LIMIT_BYTES)


def _rms_norm(x, g):
    return x * lax.rsqrt(jnp.mean(x * x, axis=-1, keepdims=True) + RMS_EPS) * g


def _sigmoid(x):
    return 1.0 / (1.0 + jnp.exp(-x))


def _head_ones():
    r = lax.broadcasted_iota(I32, (LANES, LANES), 0) // HEAD_DIM
    c = lax.broadcasted_iota(I32, (LANES, LANES), 1) // HEAD_DIM
    return (r == c).astype(BF16)


def _head_sum(x, ones):
    outs = []
    for c in range(x.shape[1] // LANES):
        xc = x[:, c * LANES:(c + 1) * LANES]
        hi = xc.astype(BF16)
        lo = (xc - hi.astype(F32)).astype(BF16)
        outs.append(jnp.dot(hi, ones, preferred_element_type=F32)
                    + jnp.dot(lo, ones, preferred_element_type=F32))
    return outs[0] if len(outs) == 1 else jnp.concatenate(outs, axis=1)


def _pool_body(x_ref, halo_ref, gmix_ref, w_ref, scale_ref, gffn_ref, h_ref, hn_ref, *,
               tiles_per_seq):
    tm = x_ref.shape[0]
    group = w_ref.shape[1]
    tile_in_seq = pl.program_id(0) % tiles_per_seq
    x = x_ref[...]
    g = gmix_ref[...]
    xn = _rms_norm(x, g)
    keep = (tile_in_seq > 0).astype(F32)
    cat = jnp.concatenate([_rms_norm(halo_ref[...], g) * keep, xn], axis=0)
    row = lax.broadcasted_iota(I32, (tm, 1), 0)
    pos = (tile_in_seq * tm + row + 1).astype(F32)
    for gi, win in enumerate(POOL_WINDOWS):
        lo = gi * group
        s = cat[:, lo:lo + group]
        k = 1
        while k < win:
            s = s + pltpu.roll(s, k, 0)
            k *= 2
        pooled = s[POOL_HALO:, :] / jnp.minimum(pos, float(win))
        d = (pooled - xn[:, lo:lo + group]).astype(BF16)
        mixed = jnp.dot(d, w_ref[gi], preferred_element_type=F32)
        h_ref[:, lo:lo + group] = x[:, lo:lo + group] + mixed * scale_ref[:, lo:lo + group]
    hn_ref[...] = _rms_norm(h_ref[...], gffn_ref[...]).astype(BF16)


def _pool_layer(h, seq, g_mix, w, scale, g_ffn):
    n, c = h.shape
    tm = ROW_TILE
    halo_blocks = tm // POOL_HALO
    return pl.pallas_call(
        functools.partial(_pool_body, tiles_per_seq=seq // tm),
        grid=(n // tm,),
        in_specs=[
            pl.BlockSpec((tm, c), lambda i: (i, 0)),
            pl.BlockSpec((POOL_HALO, c), lambda i: (jnp.maximum(i * halo_blocks - 1, 0), 0)),
            pl.BlockSpec((1, c), lambda i: (0, 0)),
            pl.BlockSpec(w.shape, lambda i: (0, 0, 0)),
            pl.BlockSpec((1, c), lambda i: (0, 0)),
            pl.BlockSpec((1, c), lambda i: (0, 0)),
        ],
        out_specs=[pl.BlockSpec((tm, c), lambda i: (i, 0)), pl.BlockSpec((tm, c), lambda i: (i, 0))],
        out_shape=[jax.ShapeDtypeStruct((n, c), F32), jax.ShapeDtypeStruct((n, c), BF16)],
        compiler_params=_params("parallel"),
        name="pool_mixer",
    )(h, h, g_mix, w, scale, g_ffn)


def _ffn_body(x_ref, wg_ref, wu_ref, wd_ref, h_ref, gnext_ref, out_ref, hn_ref):
    f = pl.program_id(1)

    @pl.when(f == 0)
    def _():
        out_ref[...] = h_ref[...]

    x = x_ref[...]
    gate = jnp.dot(x, wg_ref[...], preferred_element_type=F32)
    up = jnp.dot(x, wu_ref[...], preferred_element_type=F32)
    act = (gate * _sigmoid(gate) * up).astype(BF16)
    out_ref[...] += jnp.dot(act, wd_ref[...], preferred_element_type=F32)

    @pl.when(f == pl.num_programs(1) - 1)
    def _():
        hn_ref[...] = _rms_norm(out_ref[...], gnext_ref[...])


def _ffn_layer(hn, h, wg, wu, wd, g_next):
    n, c = h.shape
    ff = wg.shape[1]
    tm, tf = ROW_TILE, FF_TILE
    return pl.pallas_call(
        _ffn_body,
        grid=(n // tm, ff // tf),
        in_specs=[
            pl.BlockSpec((tm, c), lambda i, f: (i, 0)),
            pl.BlockSpec((c, tf), lambda i, f: (0, f)),
            pl.BlockSpec((c, tf), lambda i, f: (0, f)),
            pl.BlockSpec((tf, c), lambda i, f: (f, 0)),
            pl.BlockSpec((tm, c), lambda i, f: (i, 0)),
            pl.BlockSpec((1, c), lambda i, f: (0, 0)),
        ],
        out_specs=[pl.BlockSpec((tm, c), lambda i, f: (i, 0)),
                   pl.BlockSpec((tm, c), lambda i, f: (i, 0))],
        out_shape=[jax.ShapeDtypeStruct((n, c), F32), jax.ShapeDtypeStruct((n, c), F32)],
        compiler_params=_params("parallel", "arbitrary"),
        name="dense_swiglu",
    )(hn, wg, wu, wd, h, g_next)


def _rwkv_proj_body(*refs, tiles_per_seq, has_vres):
    if has_vres:
        (hn_ref, halo_ref, mu_ref, wr_ref, wk_ref, wv_ref, w1_ref, a1_ref, g1_ref,
         w2_ref, a2_ref, g2_ref, w0_ref, a0_ref, kk_ref, ka_ref,
         v1_ref, v2_ref, v0_ref, vf_ref,
         r_out, d_out, k_out, v_out, na_out, b_out, g_out,
         xs_ref, hw_ref, ha_ref, hg_ref, hv_ref) = refs
    else:
        (hn_ref, halo_ref, mu_ref, wr_ref, wk_ref, wv_ref, w1_ref, a1_ref, g1_ref,
         w2_ref, a2_ref, g2_ref, w0_ref, a0_ref, kk_ref, ka_ref,
         r_out, d_out, k_out, v_out, na_out, b_out, g_out,
         xs_ref, hw_ref, ha_ref, hg_ref) = refs
    tm = hn_ref.shape[0]

    @pl.when(pl.program_id(1) == 0)
    def _():
        hn = hn_ref[...]
        keep = (pl.program_id(0) % tiles_per_seq > 0).astype(F32)
        before = halo_ref[halo_ref.shape[0] - 1:, :] * keep
        row = lax.broadcasted_iota(I32, (tm, 1), 0)
        prev = jnp.where(row == 0, before, pltpu.roll(hn, 1, 0))
        xx = prev - hn
        for m in range(6):
            xs_ref[m] = (hn + xx * mu_ref[m:m + 1, :]).astype(BF16)
        hw_ref[...] = jnp.tanh(jnp.dot(xs_ref[1], w1_ref[...], preferred_element_type=F32)).astype(BF16)
        ha_ref[...] = jnp.dot(xs_ref[4], a1_ref[...], preferred_element_type=F32).astype(BF16)
        hg_ref[...] = _sigmoid(jnp.dot(xs_ref[5], g1_ref[...], preferred_element_type=F32)).astype(BF16)
        if has_vres:
            hv_ref[...] = jnp.dot(xs_ref[3], v1_ref[...], preferred_element_type=F32).astype(BF16)

    r = jnp.dot(xs_ref[0], wr_ref[...], preferred_element_type=F32)
    k = jnp.dot(xs_ref[2], wk_ref[...], preferred_element_type=F32)
    v = jnp.dot(xs_ref[3], wv_ref[...], preferred_element_type=F32)
    wl = jnp.dot(hw_ref[...], w2_ref[...], preferred_element_type=F32)
    al = jnp.dot(ha_ref[...], a2_ref[...], preferred_element_type=F32)
    gl = jnp.dot(hg_ref[...], g2_ref[...], preferred_element_type=F32)
    z = -(w0_ref[...] + wl)
    softplus = jnp.maximum(z, 0.0) + jnp.log(1.0 + jnp.exp(-jnp.abs(z)))
    w_log = -softplus - 0.5
    a = _sigmoid(a0_ref[...] + al)
    if has_vres:
        vl = jnp.dot(hv_ref[...], v2_ref[...], preferred_element_type=F32)
        v = v + (vf_ref[...] - v) * _sigmoid(v0_ref[...] + vl)
    kk = k * kk_ref[...]
    norm = jnp.sqrt(_head_sum(kk * kk, _head_ones()))
    kk = kk / jnp.maximum(norm, 1e-12)
    r_out[...] = r
    d_out[...] = jnp.exp(-jnp.exp(w_log))
    k_out[...] = k * (1.0 + (a - 1.0) * ka_ref[...])
    v_out[...] = v
    na_out[...] = -kk
    b_out[...] = kk * a
    g_out[...] = gl


def _rwkv_proj(hn, seq, p, v_first):
    n, c = hn.shape
    tm, tn = PROJ_ROWS, COL_TILE
    has_vres = v_first is not None
    halo_rows = 8
    row_spec = pl.BlockSpec((tm, c), lambda i, j: (i, 0))
    halo_spec = pl.BlockSpec((halo_rows, c),
                             lambda i, j: (jnp.maximum(i * (tm // halo_rows) - 1, 0), 0))
    full = lambda a: pl.BlockSpec(a.shape, lambda i, j: (0,) * a.ndim)
    cols = lambda a: pl.BlockSpec((a.shape[0], tn), lambda i, j: (0, j))
    tile = pl.BlockSpec((tm, tn), lambda i, j: (i, j))
    args = [hn, hn, p["mu"], p["wr"], p["wk"], p["wv"], p["w1"], p["a1"], p["g1"],
            p["w2"], p["a2"], p["g2"], p["w0"], p["a0"], p["k_k"], p["k_a"]]
    specs = [row_spec, halo_spec, full(p["mu"]), cols(p["wr"]), cols(p["wk"]), cols(p["wv"]),
             full(p["w1"]), full(p["a1"]), full(p["g1"]),
             cols(p["w2"]), cols(p["a2"]), cols(p["g2"]),
             cols(p["w0"]), cols(p["a0"]), cols(p["k_k"]), cols(p["k_a"])]
    scratch = [pltpu.VMEM((6, tm, c), BF16), pltpu.VMEM((tm, p["w1"].shape[1]), BF16),
               pltpu.VMEM((tm, p["a1"].shape[1]), BF16), pltpu.VMEM((tm, p["g1"].shape[1]), BF16)]
    if has_vres:
        args += [p["v1"], p["v2"], p["v0"], v_first]
        specs += [full(p["v1"]), cols(p["v2"]), cols(p["v0"]), tile]
        scratch.append(pltpu.VMEM((tm, p["v1"].shape[1]), BF16))
    return pl.pallas_call(
        functools.partial(_rwkv_proj_body, tiles_per_seq=seq // tm, has_vres=has_vres),
        grid=(n // tm, c // tn),
        in_specs=specs,
        out_specs=[tile] * 7,
        out_shape=[jax.ShapeDtypeStruct((n, c), F32)] * 7,
        scratch_shapes=scratch,
        compiler_params=_params("parallel", "arbitrary"),
        name="rwkv_proj_vres" if has_vres else "rwkv_proj",
    )(*args)


def _wkv_body(r_ref, d_ref, k_ref, v_ref, na_ref, b_ref, y_ref, s_ref, sb_ref, lhs_ref, wide_ref,
              big_ref):
    nb, tc, c = r_ref.shape
    width = WKV_LANES
    n_groups = c // width
    rows = n_groups * HEAD_DIM
    sub = 8
    n_chains = 1
    chain_rows = rows // n_chains
    chain_groups = n_groups // n_chains

    @pl.when(pl.program_id(0) == 0)
    def _():
        s_ref[...] = jnp.zeros_like(s_ref)
        sb_ref[...] = jnp.zeros_like(sb_ref)

    head_r = lax.broadcasted_iota(I32, (width, width), 0) // HEAD_DIM
    head_c = lax.broadcasted_iota(I32, (width, width), 1) // HEAD_DIM
    ones = (head_r == head_c).astype(BF16)
    lane_in_head = lax.broadcasted_iota(I32, (HEAD_DIM, width), 1) % HEAD_DIM
    diag = (lane_in_head == lax.broadcasted_iota(I32, (HEAD_DIM, width), 0)).astype(F32).astype(BF16)
    sub_lane = lax.broadcasted_iota(I32, (sub, width), 1) % HEAD_DIM
    sub_row = lax.broadcasted_iota(I32, (sub, width), 0)

    def tile_steps(t8, carry):
        t0 = pl.multiple_of(t8 * sub, sub)

        def row(ref, bb, s, q):
            return ref[bb, pl.ds(t0, sub), q * width:(q + 1) * width][s:s + 1, :]

        def slab(s, q):
            start = (s * n_groups + q) * HEAD_DIM
            return slice(start, start + HEAD_DIM)

        for bb in range(nb):
            for s in range(sub):
                for q in range(n_groups):
                    wide_ref[bb, slab(s, q), :] = diag * row(v_ref, bb, s, q).astype(BF16)
            big_ref[bb] = jnp.dot(wide_ref[bb], ones, preferred_element_type=F32)

        for s in range(sub):
            for bb in range(nb):
                for ch in range(n_chains):
                    base = ch * chain_rows
                    groups = range(ch * chain_groups, (ch + 1) * chain_groups)
                    for q in groups:
                        blk = slice(q * HEAD_DIM, (q + 1) * HEAD_DIM)
                        lhs_ref[bb, blk, :] = sb_ref[bb, blk, :] * row(na_ref, bb, s, q).astype(BF16)
                    red = jnp.dot(lhs_ref[bb, base:base + chain_rows, :], ones,
                                  preferred_element_type=F32)
                    for q in groups:
                        blk = slice(q * HEAD_DIM, (q + 1) * HEAD_DIM)
                        loc = slice(q * HEAD_DIM - base, (q + 1) * HEAD_DIM - base)
                        state = (s_ref[bb, blk, :] * row(d_ref, bb, s, q)
                                 + red[loc, :] * row(b_ref, bb, s, q)
                                 + big_ref[bb, slab(s, q), :] * row(k_ref, bb, s, q))
                        s_ref[bb, blk, :] = state
                        state_bf = state.astype(BF16)
                        sb_ref[bb, blk, :] = state_bf
                        wide_ref[bb, slab(s, q), :] = state_bf * row(r_ref, bb, s, q).astype(BF16)

        for bb in range(nb):
            big_ref[bb] = jnp.dot(wide_ref[bb], ones, preferred_element_type=F32)
            for q in range(n_groups):
                y_rows = []
                for s in range(sub):
                    start = (s * n_groups + q) * HEAD_DIM
                    band = big_ref[bb, start:start + sub, :]
                    for g in range(1, HEAD_DIM // sub):
                        band = jnp.where(sub_lane // sub == g,
                                         big_ref[bb, start + g * sub:start + (g + 1) * sub, :], band)
                    band = jnp.where(sub_lane % sub == sub_row, band, 0.0)
                    y_rows.append(jnp.sum(band, axis=0, keepdims=True))
                y_ref[bb, pl.ds(t0, sub), q * width:(q + 1) * width] = jnp.concatenate(y_rows, axis=0)
        return carry

    lax.fori_loop(0, tc // sub, tile_steps, 0)


def _wkv(r, d, k, v, na, b, batch):
    n, c = r.shape
    seq = n // batch
    tc = WKV_CHUNK
    rows = c // WKV_LANES * HEAD_DIM
    shaped = [a.reshape(batch, seq, c) for a in (r, d, k, v, na, b)]
    spec = pl.BlockSpec((batch, tc, c), lambda i: (0, i, 0))
    y = pl.pallas_call(
        _wkv_body,
        grid=(seq // tc,),
        in_specs=[spec] * 6,
        out_specs=spec,
        out_shape=jax.ShapeDtypeStruct((batch, seq, c), F32),
        scratch_shapes=[pltpu.VMEM((batch, rows, WKV_LANES), F32),
                        pltpu.VMEM((batch, rows, WKV_LANES), BF16),
                        pltpu.VMEM((batch, rows, WKV_LANES), BF16),
                        pltpu.VMEM((batch, 8 * rows, WKV_LANES), BF16),
                        pltpu.VMEM((batch, 8 * rows, WKV_LANES), F32)],
        compiler_params=_params("arbitrary"),
        name="wkv7_scan",
    )(*shaped)
    return y.reshape(n, c)


def _rwkv_out_body(y_ref, r_ref, k_ref, v_ref, g_ref, h_ref, lnw_ref, lnb_ref, rk_ref, wo_ref,
                   gffn_ref, router_ref, out_ref, hnp_ref, route_ref, z_ref):
    j = pl.program_id(1)
    tn = wo_ref.shape[1]

    @pl.when(j == 0)
    def _():
        ones = _head_ones()
        y = y_ref[...]
        mean = _head_sum(y, ones) * (1.0 / HEAD_DIM)
        yc = y - mean
        var = _head_sum(yc * yc, ones) * (1.0 / HEAD_DIM)
        yn = yc * lax.rsqrt(var + GN_EPS) * lnw_ref[...] + lnb_ref[...]
        bonus = _head_sum(r_ref[...] * k_ref[...] * rk_ref[...], ones) * v_ref[...]
        z_ref[...] = ((yn + bonus) * g_ref[...]).astype(BF16)

    col = pl.multiple_of(j * tn, tn)
    out_ref[:, pl.ds(col, tn)] = h_ref[:, pl.ds(col, tn)] + jnp.dot(
        z_ref[...], wo_ref[...], preferred_element_type=F32)

    @pl.when(j == pl.num_programs(1) - 1)
    def _():
        hn = _rms_norm(out_ref[...], gffn_ref[...])
        half = hn.shape[1] // 2
        lo_bits = lax.bitcast_convert_type(hn[:, :half].astype(BF16).astype(F32), U32) >> 16
        hi_bits = lax.bitcast_convert_type(hn[:, half:].astype(BF16).astype(F32), U32)
        hnp_ref[...] = (hi_bits & jnp.uint32(0xFFFF0000)) | lo_bits
        logits = jnp.dot(hn, router_ref[...], preferred_element_type=F32,
                         precision=lax.Precision.HIGHEST)
        lane = lax.broadcasted_iota(I32, logits.shape, 1)
        neg = jnp.float32(-jnp.inf)
        lg = jnp.where(lane < N_EXPERTS, logits, neg)
        m1 = jnp.max(lg, axis=1, keepdims=True)
        i1 = jnp.min(jnp.where(lg == m1, lane, LANES), axis=1, keepdims=True)
        lg2 = jnp.where(lane == i1, neg, lg)
        m2 = jnp.max(lg2, axis=1, keepdims=True)
        i2 = jnp.min(jnp.where(lg2 == m2, lane, LANES), axis=1, keepdims=True)
        e = jnp.exp(m2 - m1)
        g1 = 1.0 / (1.0 + e)
        g2 = e / (1.0 + e)
        route_ref[...] = jnp.where(
            lane == 0, i1.astype(F32),
            jnp.where(lane == 1, i2.astype(F32),
                      jnp.where(lane == 2, g1, jnp.where(lane == 3, g2, 0.0))))


def _rwkv_out(y, r, k, v, g, h, lnw, lnb, rk, wo, g_ffn, router):
    n, c = h.shape
    tm, tn = PROJ_ROWS, COL_TILE
    row = pl.BlockSpec((tm, c), lambda i, j: (i, 0))
    vec = pl.BlockSpec((1, c), lambda i, j: (0, 0))
    return pl.pallas_call(
        _rwkv_out_body,
        grid=(n // tm, c // tn),
        in_specs=[row, row, row, row, row, row, vec, vec, vec,
                  pl.BlockSpec((c, tn), lambda i, j: (0, j)), vec,
                  pl.BlockSpec(router.shape, lambda i, j: (0, 0))],
        out_specs=[row, pl.BlockSpec((tm, c // 2), lambda i, j: (i, 0)),
                   pl.BlockSpec((tm, LANES), lambda i, j: (i, 0))],
        out_shape=[jax.ShapeDtypeStruct((n, c), F32), jax.ShapeDtypeStruct((n, c // 2), U32),
                   jax.ShapeDtypeStruct((n, LANES), F32)],
        scratch_shapes=[pltpu.VMEM((tm, c), BF16)],
        compiler_params=_params("parallel", "arbitrary"),
        name="rwkv_out_router",
    )(y, r, k, v, g, h, lnw, lnb, rk, wo, g_ffn, router)


def _route_slots(route, rows_per_block):
    n = route.shape[0]
    experts = route[:, :TOP_K].astype(I32)
    flat = experts.reshape(-1)
    onehot = (flat[None, :] == jnp.arange(N_EXPERTS, dtype=I32)[:, None]).astype(I32)
    csum = jnp.cumsum(onehot, axis=1)
    counts = csum[:, -1]
    padded = (counts + rows_per_block - 1) // rows_per_block * rows_per_block
    pad_end = jnp.cumsum(padded)
    pad_start = pad_end - padded
    slot = jnp.sum(onehot * (csum - 1 + pad_start[:, None]), axis=0).reshape(n, TOP_K)
    n_blocks = (n * TOP_K) // rows_per_block + N_EXPERTS
    blk_start = jnp.arange(n_blocks, dtype=I32) * rows_per_block
    blk_expert = jnp.minimum(jnp.sum((blk_start[:, None] >= pad_end[None, :]).astype(I32), axis=1),
                             N_EXPERTS - 1)
    n_used = pad_end[-1] // rows_per_block
    return slot, blk_expert.astype(I32), n_used.astype(I32).reshape(1), n_blocks


def _dispatch_body(s0_ref, s1_ref, x_ref, xs_in_ref, xs_ref, sem):
    del xs_in_ref
    rows = x_ref.shape[0]

    def copy(r, slot):
        return pltpu.make_async_copy(x_ref.at[pl.ds(r, 1)], xs_ref.at[pl.ds(slot, 1)], sem)

    def issue(r, carry):
        copy(r, s0_ref[0, 0, r]).start()
        copy(r, s1_ref[0, 0, r]).start()
        return carry

    lax.fori_loop(0, rows, issue, 0)

    def drain(r, carry):
        copy(0, 0).wait()
        copy(0, 0).wait()
        return carry

    lax.fori_loop(0, rows, drain, 0)


def _dispatch(xp, slot, n_slots):
    n, w = xp.shape
    tr = ROUTE_ROWS
    s0 = slot[:, 0].reshape(n // tr, 1, tr)
    s1 = slot[:, 1].reshape(n // tr, 1, tr)
    smem = pl.BlockSpec((1, 1, tr), lambda i: (i, 0, 0), memory_space=pltpu.SMEM)
    return pl.pallas_call(
        _dispatch_body,
        grid=(n // tr,),
        in_specs=[smem, smem, pl.BlockSpec((tr, w), lambda i: (i, 0)),
                  pl.BlockSpec(memory_space=pl.ANY)],
        out_specs=pl.BlockSpec(memory_space=pl.ANY),
        out_shape=jax.ShapeDtypeStruct((n_slots, w), xp.dtype),
        scratch_shapes=[pltpu.SemaphoreType.DMA(())],
        input_output_aliases={3: 0},
        compiler_params=_params("arbitrary"),
        name="moe_dispatch",
    )(s0, s1, xp, jnp.zeros((n_slots, w), xp.dtype))


def _moe_body(be_ref, nu_ref, x_ref, wg_ref, wu_ref, wd_ref, y_ref, xb_ref):
    del be_ref
    b = pl.program_id(0)
    f = pl.program_id(1)

    @pl.when(f == 0)
    def _():
        y_ref[...] = jnp.zeros_like(y_ref)
        packed = x_ref[...]
        half = packed.shape[1]
        xb_ref[:, :half] = lax.bitcast_convert_type(packed << 16, F32).astype(BF16)
        xb_ref[:, half:] = lax.bitcast_convert_type(
            packed & jnp.uint32(0xFFFF0000), F32).astype(BF16)

    @pl.when(b < nu_ref[0])
    def _():
        x = xb_ref[...]
        gate = jnp.dot(x, wg_ref[...], preferred_element_type=F32)
        up = jnp.dot(x, wu_ref[...], preferred_element_type=F32)
        act = (gate * _sigmoid(gate) * up).astype(BF16)
        y_ref[...] += jnp.dot(act, wd_ref[...], preferred_element_type=F32)


def _moe_experts(xs, blk_expert, n_used, wg, wu, wd, n_blocks):
    n_slots, half = xs.shape
    c = 2 * half
    ff = wg.shape[2]
    tb, tf = MOE_ROWS, FF_TILE
    nf = ff // tf

    def used(b, nu):
        return jnp.minimum(b, nu[0] - 1)

    def f_of(b, f, nu):
        return jnp.where(b < nu[0], f, nf - 1)

    grid_spec = pltpu.PrefetchScalarGridSpec(
        num_scalar_prefetch=2,
        grid=(n_blocks, nf),
        in_specs=[
            pl.BlockSpec((tb, half), lambda b, f, be, nu: (used(b, nu), 0)),
            pl.BlockSpec((None, c, tf), lambda b, f, be, nu: (be[used(b, nu)], 0, f_of(b, f, nu))),
            pl.BlockSpec((None, c, tf), lambda b, f, be, nu: (be[used(b, nu)], 0, f_of(b, f, nu))),
            pl.BlockSpec((None, tf, c), lambda b, f, be, nu: (be[used(b, nu)], f_of(b, f, nu), 0)),
        ],
        out_specs=pl.BlockSpec((tb, c), lambda b, f, be, nu: (b, 0)),
        scratch_shapes=[pltpu.VMEM((tb, c), BF16)],
    )
    return pl.pallas_call(
        _moe_body,
        grid_spec=grid_spec,
        out_shape=jax.ShapeDtypeStruct((n_slots, c), F32),
        compiler_params=_params("arbitrary", "arbitrary"),
        name="moe_experts",
    )(blk_expert, n_used, xs, wg, wu, wd)


def _combine_body(s0_ref, s1_ref, route_ref, h_ref, g_ref, ys_ref, out_ref, buf_ref, sem, *,
                  final_norm):
    rows = h_ref.shape[0]

    def copy(r, slot, k):
        return pltpu.make_async_copy(ys_ref.at[pl.ds(slot, 1)], buf_ref.at[k, pl.ds(r, 1)], sem)

    def issue(r, carry):
        copy(r, s0_ref[0, 0, r], 0).start()
        copy(r, s1_ref[0, 0, r], 1).start()
        return carry

    lax.fori_loop(0, rows, issue, 0)

    def drain(r, carry):
        copy(0, 0, 0).wait()
        copy(0, 0, 1).wait()
        return carry

    lax.fori_loop(0, rows, drain, 0)
    route = route_ref[...]
    out = h_ref[...] + buf_ref[0] * route[:, 2:3] + buf_ref[1] * route[:, 3:4]
    if final_norm:
        out = _rms_norm(out, g_ref[...])
    out_ref[...] = out


def _combine(ys, slot, route, h, g_final, final_norm):
    n, c = h.shape
    tr = ROUTE_ROWS
    s0 = slot[:, 0].reshape(n // tr, 1, tr)
    s1 = slot[:, 1].reshape(n // tr, 1, tr)
    smem = pl.BlockSpec((1, 1, tr), lambda i: (i, 0, 0), memory_space=pltpu.SMEM)
    return pl.pallas_call(
        functools.partial(_combine_body, final_norm=final_norm),
        grid=(n // tr,),
        in_specs=[smem, smem, pl.BlockSpec((tr, LANES), lambda i: (i, 0)),
                  pl.BlockSpec((tr, c), lambda i: (i, 0)), pl.BlockSpec((1, c), lambda i: (0, 0)),
                  pl.BlockSpec(memory_space=pl.ANY)],
        out_specs=pl.BlockSpec((tr, c), lambda i: (i, 0)),
        out_shape=jax.ShapeDtypeStruct((n, c), F32),
        scratch_shapes=[pltpu.VMEM((TOP_K, tr, c), F32), pltpu.SemaphoreType.DMA(())],
        compiler_params=_params("arbitrary"),
        name="moe_combine_norm" if final_norm else "moe_combine",
    )(s0, s1, route, h, g_final, ys)


def _cast_body(w_ref, o_ref):
    o_ref[...] = w_ref[...].astype(o_ref.dtype)


def _layer_bf16(w, layer):
    cols = w.shape[-1]
    w3 = w.reshape(w.shape[0], -1, cols)
    rows = w3.shape[1]
    blk = 1 << ((CAST_BLOCK_BYTES // (cols * 4)).bit_length() - 1)
    assert rows % blk == 0, (rows, blk)
    out = pl.pallas_call(
        _cast_body,
        grid=(rows // blk,),
        in_specs=[pl.BlockSpec((None, blk, cols), lambda i: (layer, i, 0))],
        out_specs=pl.BlockSpec((blk, cols), lambda i: (i, 0)),
        out_shape=jax.ShapeDtypeStruct((rows, cols), BF16),
        compiler_params=_params("parallel"),
        name="cast_bf16",
    )(w3)
    return out.reshape(w.shape[1:])


def _pad_cols(w, mult=LANES):
    pad = -w.shape[1] % mult
    return jnp.pad(w, ((0, 0), (0, pad))) if pad else w


def _pad_rows(w, mult=LANES):
    pad = -w.shape[0] % mult
    return jnp.pad(w, ((0, pad), (0, 0))) if pad else w


def kernel(x, norm_mix, norm_ffn, norm_final, pool_w, pool_scale, rwkv_mu, rwkv_w0, rwkv_w1, rwkv_w2, rwkv_a0, rwkv_a1, rwkv_a2, rwkv_v0, rwkv_v1, rwkv_v2, rwkv_g1, rwkv_g2, rwkv_k_k, rwkv_k_a, rwkv_r_k, rwkv_wr, rwkv_wk, rwkv_wv, rwkv_wo, rwkv_lnx_w, rwkv_lnx_b, ffn_w_gate, ffn_w_up, ffn_w_down, moe_router, moe_w_gate, moe_w_up, moe_w_down):
    batch, seq, c = x.shape
    n = batch * seq
    depth = norm_mix.shape[0]
    vec = lambda a: a.reshape(1, c)
    h = x.reshape(n, c)
    hn = None
    v_first = None
    for i in range(depth):
        j = i // 2
        if i % 2 == 0:
            h, hn_ffn = _pool_layer(h, seq, vec(norm_mix[i]), pool_w[j].astype(BF16),
                                    vec(pool_scale[j]), vec(norm_ffn[i]))
            h, hn = _ffn_layer(hn_ffn, h, _layer_bf16(ffn_w_gate, j), _layer_bf16(ffn_w_up, j),
                               _layer_bf16(ffn_w_down, j), vec(norm_mix[i + 1]))
        else:
            p = {
                "mu": jnp.pad(rwkv_mu[j], ((0, 2), (0, 0))),
                "wr": _layer_bf16(rwkv_wr, j), "wk": _layer_bf16(rwkv_wk, j),
                "wv": _layer_bf16(rwkv_wv, j),
                "w1": _pad_cols(rwkv_w1[j]).astype(BF16), "w2": _pad_rows(rwkv_w2[j]).astype(BF16),
                "a1": _pad_cols(rwkv_a1[j]).astype(BF16), "a2": _pad_rows(rwkv_a2[j]).astype(BF16),
                "g1": _pad_cols(rwkv_g1[j]).astype(BF16), "g2": _pad_rows(rwkv_g2[j]).astype(BF16),
                "w0": vec(rwkv_w0[j]), "a0": vec(rwkv_a0[j]),
                "k_k": vec(rwkv_k_k[j]), "k_a": vec(rwkv_k_a[j]),
            }
            if j > 0:
                p["v1"] = _pad_cols(rwkv_v1[j - 1]).astype(BF16)
                p["v2"] = _pad_rows(rwkv_v2[j - 1]).astype(BF16)
                p["v0"] = vec(rwkv_v0[j - 1])
            r, d, k, v, na, b, g = _rwkv_proj(hn, seq, p, v_first if j > 0 else None)
            if v_first is None:
                v_first = v
            y = _wkv(r, d, k, v, na, b, batch)
            h, hn_packed, route = _rwkv_out(
                y, r, k, v, g, h, vec(rwkv_lnx_w[j]), vec(rwkv_lnx_b[j]), vec(rwkv_r_k[j]),
                _layer_bf16(rwkv_wo, j), vec(norm_ffn[i]), _pad_cols(moe_router[j]))
            slot, blk_expert, n_used, n_blocks = _route_slots(route, MOE_ROWS)
            xs = _dispatch(hn_packed, slot, n_blocks * MOE_ROWS)
            ys = _moe_experts(xs, blk_expert, n_used, _layer_bf16(moe_w_gate, j),
                              _layer_bf16(moe_w_up, j), _layer_bf16(moe_w_down, j), n_blocks)
            h = _combine(ys, slot, route, h, vec(norm_final), final_norm=(i == depth - 1))
    return h.reshape(batch, seq, c)
```

```python
import functools

import jax
import jax.numpy as jnp
from jax import lax
from jax.experimental import pallas as pl
from jax.experimental.pallas import tpu as pltpu

F32 = jnp.float32
BF16 = jnp.bfloat16
I32 = jnp.int32
U32 = jnp.uint32

LANES = 128
HEAD_DIM = 64
HEADS_PER_VREG = LANES // HEAD_DIM
RMS_EPS = 1e-6
GN_EPS = 64e-5
POOL_WINDOWS = (2, 4, 8, 16)
POOL_HALO = 16
N_EXPERTS = 8
TOP_K = 2
VMEM_LIMIT_BYTES = 56 * 1024 * 1024

ROW_TILE = 512
PROJ_ROWS = 256
COL_TILE = 512
FF_TILE = 512
MOE_FF_TILE = 1024
MOE_ROWS = 512
ROUTE_ROWS = 256
DMA_UNROLL = 8
CAST_BLOCK_BYTES = 8 * 1024 * 1024
WKV_CHUNK = 64
WKV_LANES = 256
WKV_TILE = 8


def _params(*sem):
    return pltpu.CompilerParams(dimension_semantics=sem, vmem_limit_bytes=VMEM_LIMIT_BYTES)


def _rms_norm(x, g):
    return x * lax.rsqrt(jnp.mean(x * x, axis=-1, keepdims=True) + RMS_EPS) * g


def _sigmoid(x):
    return 1.0 / (1.0 + jnp.exp(-x))


def _head_ones():
    r = lax.broadcasted_iota(I32, (LANES, LANES), 0) // HEAD_DIM
    c = lax.broadcasted_iota(I32, (LANES, LANES), 1) // HEAD_DIM
    return (r == c).astype(BF16)


def _head_sum(x, ones):
    outs = []
    for c in range(x.shape[1] // LANES):
        xc = x[:, c * LANES:(c + 1) * LANES]
        hi = xc.astype(BF16)
        lo = (xc - hi.astype(F32)).astype(BF16)
        outs.append(jnp.dot(hi, ones, preferred_element_type=F32)
                    + jnp.dot(lo, ones, preferred_element_type=F32))
    return outs[0] if len(outs) == 1 else jnp.concatenate(outs, axis=1)


def _pool_body(x_ref, halo_ref, gmix_ref, w_ref, scale_ref, gffn_ref, h_ref, hn_ref, *,
               tiles_per_seq):
    tm = x_ref.shape[0]
    group = w_ref.shape[1]
    tile_in_seq = pl.program_id(0) % tiles_per_seq
    x = x_ref[...]
    g = gmix_ref[...]
    xn = _rms_norm(x, g)
    keep = (tile_in_seq > 0).astype(F32)
    cat = jnp.concatenate([_rms_norm(halo_ref[...], g) * keep, xn], axis=0)
    row = lax.broadcasted_iota(I32, (tm, 1), 0)
    pos = (tile_in_seq * tm + row + 1).astype(F32)
    for gi, win in enumerate(POOL_WINDOWS):
        lo = gi * group
        s = cat[:, lo:lo + group]
        k = 1
        while k < win:
            s = s + pltpu.roll(s, k, 0)
            k *= 2
        pooled = s[POOL_HALO:, :] / jnp.minimum(pos, float(win))
        d = (pooled - xn[:, lo:lo + group]).astype(BF16)
        mixed = jnp.dot(d, w_ref[gi], preferred_element_type=F32)
        h_ref[:, lo:lo + group] = x[:, lo:lo + group] + mixed * scale_ref[:, lo:lo + group]
    hn_ref[...] = _rms_norm(h_ref[...], gffn_ref[...]).astype(BF16)


def _pool_layer(h, seq, g_mix, w, scale, g_ffn):
    n, c = h.shape
    tm = ROW_TILE
    halo_blocks = tm // POOL_HALO
    return pl.pallas_call(
        functools.partial(_pool_body, tiles_per_seq=seq // tm),
        grid=(n // tm,),
        in_specs=[
            pl.BlockSpec((tm, c), lambda i: (i, 0)),
            pl.BlockSpec((POOL_HALO, c), lambda i: (jnp.maximum(i * halo_blocks - 1, 0), 0)),
            pl.BlockSpec((1, c), lambda i: (0, 0)),
            pl.BlockSpec(w.shape, lambda i: (0, 0, 0)),
            pl.BlockSpec((1, c), lambda i: (0, 0)),
            pl.BlockSpec((1, c), lambda i: (0, 0)),
        ],
        out_specs=[pl.BlockSpec((tm, c), lambda i: (i, 0)), pl.BlockSpec((tm, c), lambda i: (i, 0))],
        out_shape=[jax.ShapeDtypeStruct((n, c), F32), jax.ShapeDtypeStruct((n, c), BF16)],
        compiler_params=_params("parallel"),
        name="pool_mixer",
    )(h, h, g_mix, w, scale, g_ffn)


def _ffn_body(x_ref, wg_ref, wu_ref, wd_ref, h_ref, gnext_ref, out_ref, hn_ref):
    f = pl.program_id(1)

    @pl.when(f == 0)
    def _():
        out_ref[...] = h_ref[...]

    x = x_ref[...]
    gate = jnp.dot(x, wg_ref[...], preferred_element_type=F32)
    up = jnp.dot(x, wu_ref[...], preferred_element_type=F32)
    act = (gate * _sigmoid(gate) * up).astype(BF16)
    out_ref[...] += jnp.dot(act, wd_ref[...], preferred_element_type=F32)

    @pl.when(f == pl.num_programs(1) - 1)
    def _():
        hn_ref[...] = _rms_norm(out_ref[...], gnext_ref[...])


def _ffn_layer(hn, h, wg, wu, wd, g_next):
    n, c = h.shape
    ff = wg.shape[1]
    tm, tf = ROW_TILE, FF_TILE
    return pl.pallas_call(
        _ffn_body,
        grid=(n // tm, ff // tf),
        in_specs=[
            pl.BlockSpec((tm, c), lambda i, f: (i, 0)),
            pl.BlockSpec((c, tf), lambda i, f: (0, f)),
            pl.BlockSpec((c, tf), lambda i, f: (0, f)),
            pl.BlockSpec((tf, c), lambda i, f: (f, 0)),
            pl.BlockSpec((tm, c), lambda i, f: (i, 0)),
            pl.BlockSpec((1, c), lambda i, f: (0, 0)),
        ],
        out_specs=[pl.BlockSpec((tm, c), lambda i, f: (i, 0)),
                   pl.BlockSpec((tm, c), lambda i, f: (i, 0))],
        out_shape=[jax.ShapeDtypeStruct((n, c), F32), jax.ShapeDtypeStruct((n, c), F32)],
        compiler_params=_params("parallel", "arbitrary"),
        name="dense_swiglu",
    )(hn, wg, wu, wd, h, g_next)


def _rwkv_proj_body(*refs, tiles_per_seq, has_vres):
    if has_vres:
        (hn_ref, halo_ref, mu_ref, wr_ref, wk_ref, wv_ref, w1_ref, a1_ref, g1_ref,
         w2_ref, a2_ref, g2_ref, w0_ref, a0_ref, kk_ref, ka_ref,
         v1_ref, v2_ref, v0_ref, vf_ref,
         r_out, d_out, k_out, v_out, na_out, b_out, g_out,
         xs_ref, hw_ref, ha_ref, hg_ref, hv_ref) = refs
    else:
        (hn_ref, halo_ref, mu_ref, wr_ref, wk_ref, wv_ref, w1_ref, a1_ref, g1_ref,
         w2_ref, a2_ref, g2_ref, w0_ref, a0_ref, kk_ref, ka_ref,
         r_out, d_out, k_out, v_out, na_out, b_out, g_out,
         xs_ref, hw_ref, ha_ref, hg_ref) = refs
    tm = hn_ref.shape[0]

    @pl.when(pl.program_id(1) == 0)
    def _():
        hn = hn_ref[...]
        keep = (pl.program_id(0) % tiles_per_seq > 0).astype(F32)
        before = halo_ref[halo_ref.shape[0] - 1:, :] * keep
        row = lax.broadcasted_iota(I32, (tm, 1), 0)
        prev = jnp.where(row == 0, before, pltpu.roll(hn, 1, 0))
        xx = prev - hn
        for m in range(6):
            xs_ref[m] = (hn + xx * mu_ref[m:m + 1, :]).astype(BF16)
        hw_ref[...] = jnp.tanh(jnp.dot(xs_ref[1], w1_ref[...], preferred_element_type=F32)).astype(BF16)
        ha_ref[...] = jnp.dot(xs_ref[4], a1_ref[...], preferred_element_type=F32).astype(BF16)
        hg_ref[...] = _sigmoid(jnp.dot(xs_ref[5], g1_ref[...], preferred_element_type=F32)).astype(BF16)
        if has_vres:
            hv_ref[...] = jnp.dot(xs_ref[3], v1_ref[...], preferred_element_type=F32).astype(BF16)

    r = jnp.dot(xs_ref[0], wr_ref[...], preferred_element_type=F32)
    k = jnp.dot(xs_ref[2], wk_ref[...], preferred_element_type=F32)
    v = jnp.dot(xs_ref[3], wv_ref[...], preferred_element_type=F32)
    wl = jnp.dot(hw_ref[...], w2_ref[...], preferred_element_type=F32)
    al = jnp.dot(ha_ref[...], a2_ref[...], preferred_element_type=F32)
    gl = jnp.dot(hg_ref[...], g2_ref[...], preferred_element_type=F32)
    z = -(w0_ref[...] + wl)
    softplus = jnp.maximum(z, 0.0) + jnp.log(1.0 + jnp.exp(-jnp.abs(z)))
    w_log = -softplus - 0.5
    a = _sigmoid(a0_ref[...] + al)
    if has_vres:
        vl = jnp.dot(hv_ref[...], v2_ref[...], preferred_element_type=F32)
        v = v + (vf_ref[...] - v) * _sigmoid(v0_ref[...] + vl)
    kk = k * kk_ref[...]
    norm = jnp.sqrt(_head_sum(kk * kk, _head_ones()))
    kk = kk / jnp.maximum(norm, 1e-12)
    r_out[...] = r
    d_out[...] = -jnp.exp(w_log)
    k_out[...] = k * (1.0 + (a - 1.0) * ka_ref[...])
    v_out[...] = v
    na_out[...] = -kk
    b_out[...] = kk * a
    g_out[...] = gl


def _rwkv_proj(hn, seq, p, v_first):
    n, c = hn.shape
    tm, tn = PROJ_ROWS, COL_TILE
    has_vres = v_first is not None
    halo_rows = 8
    row_spec = pl.BlockSpec((tm, c), lambda i, j: (i, 0))
    halo_spec = pl.BlockSpec((halo_rows, c),
                             lambda i, j: (jnp.maximum(i * (tm // halo_rows) - 1, 0), 0))
    full = lambda a: pl.BlockSpec(a.shape, lambda i, j: (0,) * a.ndim)
    cols = lambda a: pl.BlockSpec((a.shape[0], tn), lambda i, j: (0, j))
    tile = pl.BlockSpec((tm, tn), lambda i, j: (i, j))
    args = [hn, hn, p["mu"], p["wr"], p["wk"], p["wv"], p["w1"], p["a1"], p["g1"],
            p["w2"], p["a2"], p["g2"], p["w0"], p["a0"], p["k_k"], p["k_a"]]
    specs = [row_spec, halo_spec, full(p["mu"]), cols(p["wr"]), cols(p["wk"]), cols(p["wv"]),
             full(p["w1"]), full(p["a1"]), full(p["g1"]),
             cols(p["w2"]), cols(p["a2"]), cols(p["g2"]),
             cols(p["w0"]), cols(p["a0"]), cols(p["k_k"]), cols(p["k_a"])]
    scratch = [pltpu.VMEM((6, tm, c), BF16), pltpu.VMEM((tm, p["w1"].shape[1]), BF16),
               pltpu.VMEM((tm, p["a1"].shape[1]), BF16), pltpu.VMEM((tm, p["g1"].shape[1]), BF16)]
    if has_vres:
        args += [p["v1"], p["v2"], p["v0"], v_first]
        specs += [full(p["v1"]), cols(p["v2"]), cols(p["v0"]), tile]
        scratch.append(pltpu.VMEM((tm, p["v1"].shape[1]), BF16))
    return pl.pallas_call(
        functools.partial(_rwkv_proj_body, tiles_per_seq=seq // tm, has_vres=has_vres),
        grid=(n // tm, c // tn),
        in_specs=specs,
        out_specs=[tile] * 7,
        out_shape=[jax.ShapeDtypeStruct((n, c), F32)] * 7,
        scratch_shapes=scratch,
        compiler_params=_params("parallel", "arbitrary"),
        name="rwkv_proj_vres" if has_vres else "rwkv_proj",
    )(*args)


def _wkv_body(r_ref, w_ref, k_ref, v_ref, na_ref, b_ref, y_ref, s_ref, sb_ref, lhs_ref, e_ref, u0_ref):
    nb, tc, c = r_ref.shape
    sub = WKV_TILE
    n_pairs = c // LANES
    n_groups = c // WKV_LANES
    kinds = 4

    @pl.when(pl.program_id(0) == 0)
    def _():
        s_ref[...] = jnp.zeros_like(s_ref)
        sb_ref[...] = jnp.zeros_like(sb_ref)

    head_r = lax.broadcasted_iota(I32, (WKV_LANES, WKV_LANES), 0) // HEAD_DIM
    head_c = lax.broadcasted_iota(I32, (WKV_LANES, WKV_LANES), 1) // HEAD_DIM
    ones = (head_r == head_c).astype(BF16)
    same_head = (lax.broadcasted_iota(I32, (LANES, LANES), 0) // HEAD_DIM
                 == lax.broadcasted_iota(I32, (LANES, LANES), 1) // HEAD_DIM)
    step = lax.broadcasted_iota(I32, (sub, c), 0)

    def shift_down(x, d):
        return jnp.where(step >= d, pltpu.roll(x, d, 0), 0.0)

    def spread(x, s):
        return jnp.broadcast_to(x[s:s + 1, :], (sub, c))

    def dots(bb, s, kind):
        start = (s * kinds + kind) * sub
        return e_ref[bb, start:start + sub, :]

    def tile_steps(t8, carry):
        t0 = pl.multiple_of(t8 * sub, sub)
        batches = range(nb)
        decay, at, rt, bt, kt, v, vs, u, us = ([None] * nb for _ in range(9))
        for bb in batches:
            load = lambda ref: ref[bb, pl.ds(t0, sub), :]
            w = load(w_ref)
            cum = w
            for d in (1, 2, 4):
                cum = cum + shift_down(cum, d)
            grow = jnp.exp(-cum)
            decay[bb] = jnp.exp(cum)
            at[bb] = load(na_ref) * jnp.exp(cum - w)
            rt[bb] = load(r_ref) * decay[bb]
            bt[bb] = load(b_ref) * grow
            kt[bb] = load(k_ref) * grow
            v[bb] = load(v_ref)
            for s in range(sub):
                bs, ks = spread(bt[bb], s), spread(kt[bb], s)
                base = s * kinds * sub
                lhs_ref[bb, base:base + 2 * sub, :] = jnp.concatenate(
                    [at[bb] * bs, at[bb] * ks], axis=0).astype(BF16)
                lhs_ref[bb, base + 2 * sub:base + 4 * sub, :] = jnp.concatenate(
                    [rt[bb] * bs, rt[bb] * ks], axis=0).astype(BF16)
            for g in range(n_groups):
                gl = slice(g * WKV_LANES, (g + 1) * WKV_LANES)
                e_ref[bb, :, gl] = jnp.dot(lhs_ref[bb, :, gl], ones, preferred_element_type=F32)
        for bb in batches:
            ar = jnp.concatenate([at[bb], rt[bb]], axis=0).astype(BF16)
            for p in range(n_pairs):
                pl_ = slice(p * LANES, (p + 1) * LANES)
                u0_ref[bb, :, pl_] = lax.dot_general(
                    ar[:, pl_], sb_ref[bb * n_pairs + p], (((1,), (1,)), ((), ())),
                    preferred_element_type=F32)
        for bb in batches:
            vs[bb] = [spread(v[bb], s) for s in range(sub)]
            u[bb] = u0_ref[bb, 0:sub, :]
            us[bb] = []
            for s in range(sub):
                us[bb].append(spread(u[bb], s))
                if s < sub - 1:
                    u[bb] = u[bb] + jnp.where(
                        step > s, dots(bb, s, 0) * us[bb][s] + dots(bb, s, 1) * vs[bb][s], 0.0)
        for bb in batches:
            uv = jnp.concatenate([u[bb], v[bb]], axis=0).astype(BF16)
            bk = jnp.concatenate([bt[bb], kt[bb]], axis=0).astype(BF16)
            for p in range(n_pairs):
                pl_ = slice(p * LANES, (p + 1) * LANES)
                idx = bb * n_pairs + p
                upd = lax.dot_general(uv[:, pl_], bk[:, pl_], (((0,), (0,)), ((), ())),
                                      preferred_element_type=F32)
                new = jnp.where(same_head, s_ref[idx] + upd, 0.0) * decay[bb][sub - 1:sub, pl_]
                s_ref[idx] = new
                sb_ref[idx] = new.astype(BF16)
        for bb in batches:
            y = u0_ref[bb, sub:2 * sub, :]
            for s in range(sub):
                y = y + jnp.where(
                    step >= s, dots(bb, s, 2) * us[bb][s] + dots(bb, s, 3) * vs[bb][s], 0.0)
            y_ref[bb, pl.ds(t0, sub), :] = y
        return carry

    lax.fori_loop(0, tc // sub, tile_steps, 0)


def _wkv(r, w, k, v, na, b, batch):
    n, c = r.shape
    seq = n // batch
    tc = WKV_CHUNK
    n_pairs = c // LANES
    shaped = [a.reshape(batch, seq, c) for a in (r, w, k, v, na, b)]
    spec = pl.BlockSpec((batch, tc, c), lambda i: (0, i, 0))
    y = pl.pallas_call(
        _wkv_body,
        grid=(seq // tc,),
        in_specs=[spec] * 6,
        out_specs=spec,
        out_shape=jax.ShapeDtypeStruct((batch, seq, c), F32),
        scratch_shapes=[pltpu.VMEM((batch * n_pairs, LANES, LANES), F32),
                        pltpu.VMEM((batch * n_pairs, LANES, LANES), BF16),
                        pltpu.VMEM((batch, 4 * WKV_TILE * WKV_TILE, c), BF16),
                        pltpu.VMEM((batch, 4 * WKV_TILE * WKV_TILE, c), F32),
                        pltpu.VMEM((batch, 2 * WKV_TILE, c), F32)],
        compiler_params=_params("arbitrary"),
        name="wkv7_scan",
    )(*shaped)
    return y.reshape(n, c)


def _rwkv_out_body(y_ref, r_ref, k_ref, v_ref, g_ref, h_ref, lnw_ref, lnb_ref, rk_ref, wo_ref,
                   gffn_ref, router_ref, out_ref, hnp_ref, route_ref, z_ref):
    j = pl.program_id(1)
    tn = wo_ref.shape[1]

    @pl.when(j == 0)
    def _():
        ones = _head_ones()
        y = y_ref[...]
        mean = _head_sum(y, ones) * (1.0 / HEAD_DIM)
        yc = y - mean
        var = _head_sum(yc * yc, ones) * (1.0 / HEAD_DIM)
        yn = yc * lax.rsqrt(var + GN_EPS) * lnw_ref[...] + lnb_ref[...]
        bonus = _head_sum(r_ref[...] * k_ref[...] * rk_ref[...], ones) * v_ref[...]
        z_ref[...] = ((yn + bonus) * g_ref[...]).astype(BF16)

    col = pl.multiple_of(j * tn, tn)
    out_ref[:, pl.ds(col, tn)] = h_ref[:, pl.ds(col, tn)] + jnp.dot(
        z_ref[...], wo_ref[...], preferred_element_type=F32)

    @pl.when(j == pl.num_programs(1) - 1)
    def _():
        hn = _rms_norm(out_ref[...], gffn_ref[...])
        half = hn.shape[1] // 2
        lo_bits = lax.bitcast_convert_type(hn[:, :half].astype(BF16).astype(F32), U32) >> 16
        hi_bits = lax.bitcast_convert_type(hn[:, half:].astype(BF16).astype(F32), U32)
        hnp_ref[...] = (hi_bits & jnp.uint32(0xFFFF0000)) | lo_bits
        logits = jnp.dot(hn, router_ref[...], preferred_element_type=F32,
                         precision=lax.Precision.HIGHEST)
        lane = lax.broadcasted_iota(I32, logits.shape, 1)
        neg = jnp.float32(-jnp.inf)
        lg = jnp.where(lane < N_EXPERTS, logits, neg)
        m1 = jnp.max(lg, axis=1, keepdims=True)
        i1 = jnp.min(jnp.where(lg == m1, lane, LANES), axis=1, keepdims=True)
        lg2 = jnp.where(lane == i1, neg, lg)
        m2 = jnp.max(lg2, axis=1, keepdims=True)
        i2 = jnp.min(jnp.where(lg2 == m2, lane, LANES), axis=1, keepdims=True)
        e = jnp.exp(m2 - m1)
        g1 = 1.0 / (1.0 + e)
        g2 = e / (1.0 + e)
        route_ref[...] = jnp.where(
            lane == 0, i1.astype(F32),
            jnp.where(lane == 1, i2.astype(F32),
                      jnp.where(lane == 2, g1, jnp.where(lane == 3, g2, 0.0))))


def _rwkv_out(y, r, k, v, g, h, lnw, lnb, rk, wo, g_ffn, router):
    n, c = h.shape
    tm, tn = PROJ_ROWS, COL_TILE
    row = pl.BlockSpec((tm, c), lambda i, j: (i, 0))
    vec = pl.BlockSpec((1, c), lambda i, j: (0, 0))
    return pl.pallas_call(
        _rwkv_out_body,
        grid=(n // tm, c // tn),
        in_specs=[row, row, row, row, row, row, vec, vec, vec,
                  pl.BlockSpec((c, tn), lambda i, j: (0, j)), vec,
                  pl.BlockSpec(router.shape, lambda i, j: (0, 0))],
        out_specs=[row, pl.BlockSpec((tm, c // 2), lambda i, j: (i, 0)),
                   pl.BlockSpec((tm, LANES), lambda i, j: (i, 0))],
        out_shape=[jax.ShapeDtypeStruct((n, c), F32), jax.ShapeDtypeStruct((n, c // 2), U32),
                   jax.ShapeDtypeStruct((n, LANES), F32)],
        scratch_shapes=[pltpu.VMEM((tm, c), BF16)],
        compiler_params=_params("parallel", "arbitrary"),
        name="rwkv_out_router",
    )(y, r, k, v, g, h, lnw, lnb, rk, wo, g_ffn, router)


def _route_slots(route, rows_per_block):
    n = route.shape[0]
    experts = route[:, :TOP_K].astype(I32)
    flat = experts.reshape(-1)
    onehot = (flat[None, :] == jnp.arange(N_EXPERTS, dtype=I32)[:, None]).astype(I32)
    csum = jnp.cumsum(onehot, axis=1)
    counts = csum[:, -1]
    padded = (counts + rows_per_block - 1) // rows_per_block * rows_per_block
    pad_end = jnp.cumsum(padded)
    pad_start = pad_end - padded
    slot = jnp.sum(onehot * (csum - 1 + pad_start[:, None]), axis=0).reshape(n, TOP_K)
    n_blocks = (n * TOP_K) // rows_per_block + N_EXPERTS
    blk_start = jnp.arange(n_blocks, dtype=I32) * rows_per_block
    blk_expert = jnp.minimum(jnp.sum((blk_start[:, None] >= pad_end[None, :]).astype(I32), axis=1),
                             N_EXPERTS - 1)
    n_used = pad_end[-1] // rows_per_block
    return slot, blk_expert.astype(I32), n_used.astype(I32).reshape(1), n_blocks


def _dispatch_body(s0_ref, s1_ref, x_ref, xs_in_ref, xs_ref, sem):
    del xs_in_ref
    rows = x_ref.shape[0]

    def copy(r, slot):
        return pltpu.make_async_copy(x_ref.at[pl.ds(r, 1)], xs_ref.at[pl.ds(slot, 1)], sem)

    def issue(r, carry):
        copy(r, s0_ref[0, 0, r]).start()
        copy(r, s1_ref[0, 0, r]).start()
        return carry

    lax.fori_loop(0, rows, issue, 0, unroll=DMA_UNROLL)

    def drain(r, carry):
        copy(0, 0).wait()
        copy(0, 0).wait()
        return carry

    lax.fori_loop(0, rows, drain, 0, unroll=DMA_UNROLL)


def _dispatch(xp, slot, n_slots):
    n, w = xp.shape
    tr = ROUTE_ROWS
    s0 = slot[:, 0].reshape(n // tr, 1, tr)
    s1 = slot[:, 1].reshape(n // tr, 1, tr)
    smem = pl.BlockSpec((1, 1, tr), lambda i: (i, 0, 0), memory_space=pltpu.SMEM)
    return pl.pallas_call(
        _dispatch_body,
        grid=(n // tr,),
        in_specs=[smem, smem, pl.BlockSpec((tr, w), lambda i: (i, 0)),
                  pl.BlockSpec(memory_space=pl.ANY)],
        out_specs=pl.BlockSpec(memory_space=pl.ANY),
        out_shape=jax.ShapeDtypeStruct((n_slots, w), xp.dtype),
        scratch_shapes=[pltpu.SemaphoreType.DMA(())],
        input_output_aliases={3: 0},
        compiler_params=_params("arbitrary"),
        name="moe_dispatch",
    )(s0, s1, xp, jnp.zeros((n_slots, w), xp.dtype))


def _moe_body(be_ref, nu_ref, x_ref, wg_ref, wu_ref, wd_ref, y_ref, xb_ref):
    del be_ref
    b = pl.program_id(0)
    f = pl.program_id(1)

    @pl.when(f == 0)
    def _():
        y_ref[...] = jnp.zeros_like(y_ref)
        packed = x_ref[...]
        half = packed.shape[1]
        xb_ref[:, :half] = lax.bitcast_convert_type(packed << 16, F32).astype(BF16)
        xb_ref[:, half:] = lax.bitcast_convert_type(
            packed & jnp.uint32(0xFFFF0000), F32).astype(BF16)

    @pl.when(b < nu_ref[0])
    def _():
        x = xb_ref[...]
        gate = jnp.dot(x, wg_ref[...], preferred_element_type=F32)
        up = jnp.dot(x, wu_ref[...], preferred_element_type=F32)
        act = (gate * _sigmoid(gate) * up).astype(BF16)
        y_ref[...] += jnp.dot(act, wd_ref[...], preferred_element_type=F32)


def _moe_experts(xs, blk_expert, n_used, wg, wu, wd, n_blocks):
    n_slots, half = xs.shape
    c = 2 * half
    ff = wg.shape[2]
    tb, tf = MOE_ROWS, MOE_FF_TILE
    nf = ff // tf

    def used(b, nu):
        return jnp.minimum(b, nu[0] - 1)

    def f_of(b, f, nu):
        return jnp.where(b < nu[0], f, nf - 1)

    grid_spec = pltpu.PrefetchScalarGridSpec(
        num_scalar_prefetch=2,
        grid=(n_blocks, nf),
        in_specs=[
            pl.BlockSpec((tb, half), lambda b, f, be, nu: (used(b, nu), 0)),
            pl.BlockSpec((None, c, tf), lambda b, f, be, nu: (be[used(b, nu)], 0, f_of(b, f, nu))),
            pl.BlockSpec((None, c, tf), lambda b, f, be, nu: (be[used(b, nu)], 0, f_of(b, f, nu))),
            pl.BlockSpec((None, tf, c), lambda b, f, be, nu: (be[used(b, nu)], f_of(b, f, nu), 0)),
        ],
        out_specs=pl.BlockSpec((tb, c), lambda b, f, be, nu: (b, 0)),
        scratch_shapes=[pltpu.VMEM((tb, c), BF16)],
    )
    return pl.pallas_call(
        _moe_body,
        grid_spec=grid_spec,
        out_shape=jax.ShapeDtypeStruct((n_slots, c), F32),
        compiler_params=_params("arbitrary", "arbitrary"),
        name="moe_experts",
    )(blk_expert, n_used, xs, wg, wu, wd)


def _combine_body(s0_ref, s1_ref, route_ref, h_ref, g_ref, ys_ref, out_ref, buf_ref, sem, *,
                  final_norm):
    rows = h_ref.shape[0]

    def copy(r, slot, k):
        return pltpu.make_async_copy(ys_ref.at[pl.ds(slot, 1)], buf_ref.at[k, pl.ds(r, 1)], sem)

    def issue(r, carry):
        copy(r, s0_ref[0, 0, r], 0).start()
        copy(r, s1_ref[0, 0, r], 1).start()
        return carry

    lax.fori_loop(0, rows, issue, 0, unroll=DMA_UNROLL)

    def drain(r, carry):
        copy(0, 0, 0).wait()
        copy(0, 0, 1).wait()
        return carry

    lax.fori_loop(0, rows, drain, 0, unroll=DMA_UNROLL)
    route = route_ref[...]
    out = h_ref[...] + buf_ref[0] * route[:, 2:3] + buf_ref[1] * route[:, 3:4]
    if final_norm:
        out = _rms_norm(out, g_ref[...])
    out_ref[...] = out


def _combine(ys, slot, route, h, g_final, final_norm):
    n, c = h.shape
    tr = ROUTE_ROWS
    s0 = slot[:, 0].reshape(n // tr, 1, tr)
    s1 = slot[:, 1].reshape(n // tr, 1, tr)
    smem = pl.BlockSpec((1, 1, tr), lambda i: (i, 0, 0), memory_space=pltpu.SMEM)
    return pl.pallas_call(
        functools.partial(_combine_body, final_norm=final_norm),
        grid=(n // tr,),
        in_specs=[smem, smem, pl.BlockSpec((tr, LANES), lambda i: (i, 0)),
                  pl.BlockSpec((tr, c), lambda i: (i, 0)), pl.BlockSpec((1, c), lambda i: (0, 0)),
                  pl.BlockSpec(memory_space=pl.ANY)],
        out_specs=pl.BlockSpec((tr, c), lambda i: (i, 0)),
        out_shape=jax.ShapeDtypeStruct((n, c), F32),
        scratch_shapes=[pltpu.VMEM((TOP_K, tr, c), F32), pltpu.SemaphoreType.DMA(())],
        compiler_params=_params("arbitrary"),
        name="moe_combine_norm" if final_norm else "moe_combine",
    )(s0, s1, route, h, g_final, ys)


def _cast_body(w_ref, o_ref):
    o_ref[...] = w_ref[...].astype(o_ref.dtype)


def _layer_bf16(w, layer):
    cols = w.shape[-1]
    w3 = w.reshape(w.shape[0], -1, cols)
    rows = w3.shape[1]
    blk = 1 << ((CAST_BLOCK_BYTES // (cols * 4)).bit_length() - 1)
    assert rows % blk == 0, (rows, blk)
    out = pl.pallas_call(
        _cast_body,
        grid=(rows // blk,),
        in_specs=[pl.BlockSpec((None, blk, cols), lambda i: (layer, i, 0))],
        out_specs=pl.BlockSpec((blk, cols), lambda i: (i, 0)),
        out_shape=jax.ShapeDtypeStruct((rows, cols), BF16),
        compiler_params=_params("parallel"),
        name="cast_bf16",
    )(w3)
    return out.reshape(w.shape[1:])


def _pad_cols(w, mult=LANES):
    pad = -w.shape[1] % mult
    return jnp.pad(w, ((0, 0), (0, pad))) if pad else w


def _pad_rows(w, mult=LANES):
    pad = -w.shape[0] % mult
    return jnp.pad(w, ((0, pad), (0, 0))) if pad else w


def kernel(x, norm_mix, norm_ffn, norm_final, pool_w, pool_scale, rwkv_mu, rwkv_w0, rwkv_w1, rwkv_w2, rwkv_a0, rwkv_a1, rwkv_a2, rwkv_v0, rwkv_v1, rwkv_v2, rwkv_g1, rwkv_g2, rwkv_k_k, rwkv_k_a, rwkv_r_k, rwkv_wr, rwkv_wk, rwkv_wv, rwkv_wo, rwkv_lnx_w, rwkv_lnx_b, ffn_w_gate, ffn_w_up, ffn_w_down, moe_router, moe_w_gate, moe_w_up, moe_w_down):
    batch, seq, c = x.shape
    n = batch * seq
    depth = norm_mix.shape[0]
    vec = lambda a: a.reshape(1, c)
    h = x.reshape(n, c)
    hn = None
    v_first = None
    for i in range(depth):
        j = i // 2
        if i % 2 == 0:
            h, hn_ffn = _pool_layer(h, seq, vec(norm_mix[i]), pool_w[j].astype(BF16),
                                    vec(pool_scale[j]), vec(norm_ffn[i]))
            h, hn = _ffn_layer(hn_ffn, h, _layer_bf16(ffn_w_gate, j), _layer_bf16(ffn_w_up, j),
                               _layer_bf16(ffn_w_down, j), vec(norm_mix[i + 1]))
        else:
            p = {
                "mu": jnp.pad(rwkv_mu[j], ((0, 2), (0, 0))),
                "wr": _layer_bf16(rwkv_wr, j), "wk": _layer_bf16(rwkv_wk, j),
                "wv": _layer_bf16(rwkv_wv, j),
                "w1": _pad_cols(rwkv_w1[j]).astype(BF16), "w2": _pad_rows(rwkv_w2[j]).astype(BF16),
                "a1": _pad_cols(rwkv_a1[j]).astype(BF16), "a2": _pad_rows(rwkv_a2[j]).astype(BF16),
                "g1": _pad_cols(rwkv_g1[j]).astype(BF16), "g2": _pad_rows(rwkv_g2[j]).astype(BF16),
                "w0": vec(rwkv_w0[j]), "a0": vec(rwkv_a0[j]),
                "k_k": vec(rwkv_k_k[j]), "k_a": vec(rwkv_k_a[j]),
            }
            if j > 0:
                p["v1"] = _pad_cols(rwkv_v1[j - 1]).astype(BF16)
                p["v2"] = _pad_rows(rwkv_v2[j - 1]).astype(BF16)
                p["v0"] = vec(rwkv_v0[j - 1])
            r, d, k, v, na, b, g = _rwkv_proj(hn, seq, p, v_first if j > 0 else None)
            if v_first is None:
                v_first = v
            y = _wkv(r, d, k, v, na, b, batch)
            h, hn_packed, route = _rwkv_out(
                y, r, k, v, g, h, vec(rwkv_lnx_w[j]), vec(rwkv_lnx_b[j]), vec(rwkv_r_k[j]),
                _layer_bf16(rwkv_wo, j), vec(norm_ffn[i]), _pad_cols(moe_router[j]))
            slot, blk_expert, n_used, n_blocks = _route_slots(route, MOE_ROWS)
            xs = _dispatch(hn_packed, slot, n_blocks * MOE_ROWS)
            ys = _moe_experts(xs, blk_expert, n_used, _layer_bf16(moe_w_gate, j),
                              _layer_bf16(moe_w_up, j), _layer_bf16(moe_w_down, j), n_blocks)
            h = _combine(ys, slot, route, h, vec(norm_final), final_norm=(i == depth - 1))
    return h.reshape(batch, seq, c)
```

```python
import functools

import jax
import jax.numpy as jnp
from jax import lax
from jax.experimental import pallas as pl
from jax.experimental.pallas import tpu as pltpu

F32 = jnp.float32
BF16 = jnp.bfloat16
I32 = jnp.int32
U32 = jnp.uint32

LANES = 128
HEAD_DIM = 64
HEADS_PER_VREG = LANES // HEAD_DIM
RMS_EPS = 1e-6
GN_EPS = 64e-5
POOL_WINDOWS = (2, 4, 8, 16)
POOL_HALO = 16
N_EXPERTS = 8
TOP_K = 2
VMEM_LIMIT_BYTES = 56 * 1024 * 1024

ROW_TILE = 512
PROJ_ROWS = 512
OUT_ROWS = 256
COL_TILE = 512
FF_TILE = 512
MOE_FF_TILE = 512
MOE_ROWS = 768
ROUTE_ROWS = 256
DMA_UNROLL = 8
CAST_BLOCK_BYTES = 8 * 1024 * 1024
WKV_CHUNK = 64
WKV_LANES = 256
WKV_TILE = 8


def _params(*sem):
    return pltpu.CompilerParams(dimension_semantics=sem, vmem_limit_bytes=VMEM_LIMIT_BYTES)


def _rms_norm(x, g):
    return x * lax.rsqrt(jnp.mean(x * x, axis=-1, keepdims=True) + RMS_EPS) * g


def _sigmoid(x):
    return 1.0 / (1.0 + jnp.exp(-x))


def _head_ones():
    r = lax.broadcasted_iota(I32, (LANES, LANES), 0) // HEAD_DIM
    c = lax.broadcasted_iota(I32, (LANES, LANES), 1) // HEAD_DIM
    return (r == c).astype(BF16)


def _head_sum(x, ones):
    outs = []
    for c in range(x.shape[1] // LANES):
        xc = x[:, c * LANES:(c + 1) * LANES]
        hi = xc.astype(BF16)
        lo = (xc - hi.astype(F32)).astype(BF16)
        outs.append(jnp.dot(hi, ones, preferred_element_type=F32)
                    + jnp.dot(lo, ones, preferred_element_type=F32))
    return outs[0] if len(outs) == 1 else jnp.concatenate(outs, axis=1)


def _pool_body(x_ref, halo_ref, gmix_ref, w_ref, scale_ref, gffn_ref, h_ref, hn_ref, *,
               tiles_per_seq):
    tm = x_ref.shape[0]
    group = w_ref.shape[1]
    tile_in_seq = pl.program_id(0) % tiles_per_seq
    x = x_ref[...]
    g = gmix_ref[...]
    xn = _rms_norm(x, g)
    keep = (tile_in_seq > 0).astype(F32)
    cat = jnp.concatenate([_rms_norm(halo_ref[...], g) * keep, xn], axis=0)
    row = lax.broadcasted_iota(I32, (tm, 1), 0)
    pos = (tile_in_seq * tm + row + 1).astype(F32)
    for gi, win in enumerate(POOL_WINDOWS):
        lo = gi * group
        s = cat[:, lo:lo + group]
        k = 1
        while k < win:
            s = s + pltpu.roll(s, k, 0)
            k *= 2
        pooled = s[POOL_HALO:, :] / jnp.minimum(pos, float(win))
        d = (pooled - xn[:, lo:lo + group]).astype(BF16)
        mixed = jnp.dot(d, w_ref[gi], preferred_element_type=F32)
        h_ref[:, lo:lo + group] = x[:, lo:lo + group] + mixed * scale_ref[:, lo:lo + group]
    hn_ref[...] = _rms_norm(h_ref[...], gffn_ref[...]).astype(BF16)


def _pool_layer(h, seq, g_mix, w, scale, g_ffn):
    n, c = h.shape
    tm = ROW_TILE
    halo_blocks = tm // POOL_HALO
    return pl.pallas_call(
        functools.partial(_pool_body, tiles_per_seq=seq // tm),
        grid=(n // tm,),
        in_specs=[
            pl.BlockSpec((tm, c), lambda i: (i, 0)),
            pl.BlockSpec((POOL_HALO, c), lambda i: (jnp.maximum(i * halo_blocks - 1, 0), 0)),
            pl.BlockSpec((1, c), lambda i: (0, 0)),
            pl.BlockSpec(w.shape, lambda i: (0, 0, 0)),
            pl.BlockSpec((1, c), lambda i: (0, 0)),
            pl.BlockSpec((1, c), lambda i: (0, 0)),
        ],
        out_specs=[pl.BlockSpec((tm, c), lambda i: (i, 0)), pl.BlockSpec((tm, c), lambda i: (i, 0))],
        out_shape=[jax.ShapeDtypeStruct((n, c), F32), jax.ShapeDtypeStruct((n, c), BF16)],
        compiler_params=_params("parallel"),
        name="pool_mixer",
    )(h, h, g_mix, w, scale, g_ffn)


def _ffn_body(x_ref, wg_ref, wu_ref, wd_ref, h_ref, gnext_ref, out_ref, hn_ref):
    f = pl.program_id(1)

    @pl.when(f == 0)
    def _():
        out_ref[...] = h_ref[...]

    x = x_ref[...]
    gate = jnp.dot(x, wg_ref[...], preferred_element_type=F32)
    up = jnp.dot(x, wu_ref[...], preferred_element_type=F32)
    act = (gate * _sigmoid(gate) * up).astype(BF16)
    out_ref[...] += jnp.dot(act, wd_ref[...], preferred_element_type=F32)

    @pl.when(f == pl.num_programs(1) - 1)
    def _():
        hn_ref[...] = _rms_norm(out_ref[...], gnext_ref[...])


def _ffn_layer(hn, h, wg, wu, wd, g_next):
    n, c = h.shape
    ff = wg.shape[1]
    tm, tf = ROW_TILE, FF_TILE
    return pl.pallas_call(
        _ffn_body,
        grid=(n // tm, ff // tf),
        in_specs=[
            pl.BlockSpec((tm, c), lambda i, f: (i, 0)),
            pl.BlockSpec((c, tf), lambda i, f: (0, f)),
            pl.BlockSpec((c, tf), lambda i, f: (0, f)),
            pl.BlockSpec((tf, c), lambda i, f: (f, 0)),
            pl.BlockSpec((tm, c), lambda i, f: (i, 0)),
            pl.BlockSpec((1, c), lambda i, f: (0, 0)),
        ],
        out_specs=[pl.BlockSpec((tm, c), lambda i, f: (i, 0)),
                   pl.BlockSpec((tm, c), lambda i, f: (i, 0))],
        out_shape=[jax.ShapeDtypeStruct((n, c), F32), jax.ShapeDtypeStruct((n, c), F32)],
        compiler_params=_params("parallel", "arbitrary"),
        name="dense_swiglu",
    )(hn, wg, wu, wd, h, g_next)


def _rwkv_proj_body(*refs, tiles_per_seq, has_vres):
    if has_vres:
        (hn_ref, halo_ref, mu_ref, wr_ref, wk_ref, wv_ref, w1_ref, a1_ref, g1_ref,
         w2_ref, a2_ref, g2_ref, w0_ref, a0_ref, kk_ref, ka_ref,
         v1_ref, v2_ref, v0_ref, vf_ref,
         r_out, d_out, k_out, v_out, na_out, b_out, g_out,
         xs_ref, hw_ref, ha_ref, hg_ref, hv_ref) = refs
    else:
        (hn_ref, halo_ref, mu_ref, wr_ref, wk_ref, wv_ref, w1_ref, a1_ref, g1_ref,
         w2_ref, a2_ref, g2_ref, w0_ref, a0_ref, kk_ref, ka_ref,
         r_out, d_out, k_out, v_out, na_out, b_out, g_out,
         xs_ref, hw_ref, ha_ref, hg_ref) = refs
    tm = hn_ref.shape[0]

    @pl.when(pl.program_id(1) == 0)
    def _():
        hn = hn_ref[...]
        keep = (pl.program_id(0) % tiles_per_seq > 0).astype(F32)
        before = halo_ref[halo_ref.shape[0] - 1:, :] * keep
        row = lax.broadcasted_iota(I32, (tm, 1), 0)
        prev = jnp.where(row == 0, before, pltpu.roll(hn, 1, 0))
        xx = prev - hn
        for m in range(6):
            xs_ref[m] = (hn + xx * mu_ref[m:m + 1, :]).astype(BF16)
        hw_ref[...] = jnp.tanh(jnp.dot(xs_ref[1], w1_ref[...], preferred_element_type=F32)).astype(BF16)
        ha_ref[...] = jnp.dot(xs_ref[4], a1_ref[...], preferred_element_type=F32).astype(BF16)
        hg_ref[...] = _sigmoid(jnp.dot(xs_ref[5], g1_ref[...], preferred_element_type=F32)).astype(BF16)
        if has_vres:
            hv_ref[...] = jnp.dot(xs_ref[3], v1_ref[...], preferred_element_type=F32).astype(BF16)

    r = jnp.dot(xs_ref[0], wr_ref[...], preferred_element_type=F32)
    k = jnp.dot(xs_ref[2], wk_ref[...], preferred_element_type=F32)
    v = jnp.dot(xs_ref[3], wv_ref[...], preferred_element_type=F32)
    wl = jnp.dot(hw_ref[...], w2_ref[...], preferred_element_type=F32)
    al = jnp.dot(ha_ref[...], a2_ref[...], preferred_element_type=F32)
    gl = jnp.dot(hg_ref[...], g2_ref[...], preferred_element_type=F32)
    z = -(w0_ref[...] + wl)
    softplus = jnp.maximum(z, 0.0) + jnp.log(1.0 + jnp.exp(-jnp.abs(z)))
    w_log = -softplus - 0.5
    a = _sigmoid(a0_ref[...] + al)
    if has_vres:
        vl = jnp.dot(hv_ref[...], v2_ref[...], preferred_element_type=F32)
        v = v + (vf_ref[...].astype(F32) - v) * _sigmoid(v0_ref[...] + vl)
    kk = k * kk_ref[...]
    norm = jnp.sqrt(_head_sum(kk * kk, _head_ones()))
    kk = kk / jnp.maximum(norm, 1e-12)
    r_out[...] = r.astype(r_out.dtype)
    d_out[...] = -jnp.exp(w_log)
    k_out[...] = (k * (1.0 + (a - 1.0) * ka_ref[...])).astype(k_out.dtype)
    v_out[...] = v.astype(v_out.dtype)
    na_out[...] = (-kk).astype(na_out.dtype)
    b_out[...] = (kk * a).astype(b_out.dtype)
    g_out[...] = gl.astype(g_out.dtype)


def _rwkv_proj(hn, seq, p, v_first):
    n, c = hn.shape
    tm, tn = PROJ_ROWS, COL_TILE
    has_vres = v_first is not None
    halo_rows = 8
    row_spec = pl.BlockSpec((tm, c), lambda i, j: (i, 0))
    halo_spec = pl.BlockSpec((halo_rows, c),
                             lambda i, j: (jnp.maximum(i * (tm // halo_rows) - 1, 0), 0))
    full = lambda a: pl.BlockSpec(a.shape, lambda i, j: (0,) * a.ndim)
    cols = lambda a: pl.BlockSpec((a.shape[0], tn), lambda i, j: (0, j))
    tile = pl.BlockSpec((tm, tn), lambda i, j: (i, j))
    args = [hn, hn, p["mu"], p["wr"], p["wk"], p["wv"], p["w1"], p["a1"], p["g1"],
            p["w2"], p["a2"], p["g2"], p["w0"], p["a0"], p["k_k"], p["k_a"]]
    specs = [row_spec, halo_spec, full(p["mu"]), cols(p["wr"]), cols(p["wk"]), cols(p["wv"]),
             full(p["w1"]), full(p["a1"]), full(p["g1"]),
             cols(p["w2"]), cols(p["a2"]), cols(p["g2"]),
             cols(p["w0"]), cols(p["a0"]), cols(p["k_k"]), cols(p["k_a"])]
    scratch = [pltpu.VMEM((6, tm, c), BF16), pltpu.VMEM((tm, p["w1"].shape[1]), BF16),
               pltpu.VMEM((tm, p["a1"].shape[1]), BF16), pltpu.VMEM((tm, p["g1"].shape[1]), BF16)]
    if has_vres:
        args += [p["v1"], p["v2"], p["v0"], v_first]
        specs += [full(p["v1"]), cols(p["v2"]), cols(p["v0"]), tile]
        scratch.append(pltpu.VMEM((tm, p["v1"].shape[1]), BF16))
    return pl.pallas_call(
        functools.partial(_rwkv_proj_body, tiles_per_seq=seq // tm, has_vres=has_vres),
        grid=(n // tm, c // tn),
        in_specs=specs,
        out_specs=[tile] * 7,
        out_shape=[jax.ShapeDtypeStruct((n, c), F32 if i == 1 else BF16) for i in range(7)],
        scratch_shapes=scratch,
        compiler_params=_params("parallel", "arbitrary"),
        name="rwkv_proj_vres" if has_vres else "rwkv_proj",
    )(*args)


def _wkv_body(r_ref, w_ref, k_ref, v_ref, na_ref, b_ref, y_ref, s_ref, sb_ref, lhs_ref, e_ref, u0_ref):
    nb, tc, c = r_ref.shape
    sub = WKV_TILE
    pack = 2 * sub
    n_pairs = c // LANES
    n_groups = c // WKV_LANES
    kinds = 4

    @pl.when(pl.program_id(0) == 0)
    def _():
        s_ref[...] = jnp.zeros_like(s_ref)
        sb_ref[...] = jnp.zeros_like(sb_ref)

    head_r = lax.broadcasted_iota(I32, (WKV_LANES, WKV_LANES), 0) // HEAD_DIM
    head_c = lax.broadcasted_iota(I32, (WKV_LANES, WKV_LANES), 1) // HEAD_DIM
    ones = (head_r == head_c).astype(BF16)
    same_head = (lax.broadcasted_iota(I32, (LANES, LANES), 0) // HEAD_DIM
                 == lax.broadcasted_iota(I32, (LANES, LANES), 1) // HEAD_DIM)
    step = lax.broadcasted_iota(I32, (sub, c), 0)

    def shift_down(x, d):
        return jnp.where(step >= d, pltpu.roll(x, d, 0), 0.0)

    def spread(x, s):
        return jnp.broadcast_to(x[s:s + 1, :], (sub, c))

    def dots(bb, s, kind):
        start = (s * kinds + kind) * sub
        return e_ref[bb, start:start + sub, :]

    def solve_tile(t_pack, half):
        t0 = pl.multiple_of(t_pack + half * sub, sub)
        rows = slice(half * sub, (half + 1) * sub)
        batches = range(nb)
        decay, at, rt, bt, kt, v, vs, u, us = ([None] * nb for _ in range(9))
        for bb in batches:
            load = lambda ref: ref[bb, pl.ds(t_pack, pack), :].astype(F32)[rows, :]
            w = w_ref[bb, pl.ds(t0, sub), :]
            cum = w
            for d in (1, 2, 4):
                cum = cum + shift_down(cum, d)
            grow = jnp.exp(-cum)
            decay[bb] = jnp.exp(cum)
            at[bb] = load(na_ref) * jnp.exp(cum - w)
            rt[bb] = load(r_ref) * decay[bb]
            bt[bb] = load(b_ref) * grow
            kt[bb] = load(k_ref) * grow
            v[bb] = load(v_ref)
            for s in range(sub):
                bs, ks = spread(bt[bb], s), spread(kt[bb], s)
                base = s * kinds * sub
                lhs_ref[bb, base:base + 2 * sub, :] = jnp.concatenate(
                    [at[bb] * bs, at[bb] * ks], axis=0).astype(BF16)
                lhs_ref[bb, base + 2 * sub:base + 4 * sub, :] = jnp.concatenate(
                    [rt[bb] * bs, rt[bb] * ks], axis=0).astype(BF16)
            for g in range(n_groups):
                gl = slice(g * WKV_LANES, (g + 1) * WKV_LANES)
                e_ref[bb, :, gl] = jnp.dot(lhs_ref[bb, :, gl], ones, preferred_element_type=F32)
        for bb in batches:
            ar = jnp.concatenate([at[bb], rt[bb]], axis=0).astype(BF16)
            for p in range(n_pairs):
                pl_ = slice(p * LANES, (p + 1) * LANES)
                u0_ref[bb, :, pl_] = lax.dot_general(
                    ar[:, pl_], sb_ref[bb * n_pairs + p], (((1,), (1,)), ((), ())),
                    preferred_element_type=F32)
        for bb in batches:
            vs[bb] = [spread(v[bb], s) for s in range(sub)]
            u[bb] = u0_ref[bb, 0:sub, :]
            us[bb] = []
            for s in range(sub):
                us[bb].append(spread(u[bb], s))
                if s < sub - 1:
                    u[bb] = u[bb] + jnp.where(
                        step > s, dots(bb, s, 0) * us[bb][s] + dots(bb, s, 1) * vs[bb][s], 0.0)
        for bb in batches:
            uv = jnp.concatenate([u[bb], v[bb]], axis=0).astype(BF16)
            bk = jnp.concatenate([bt[bb], kt[bb]], axis=0).astype(BF16)
            for p in range(n_pairs):
                pl_ = slice(p * LANES, (p + 1) * LANES)
                idx = bb * n_pairs + p
                upd = lax.dot_general(uv[:, pl_], bk[:, pl_], (((0,), (0,)), ((), ())),
                                      preferred_element_type=F32)
                new = jnp.where(same_head, s_ref[idx] + upd, 0.0) * decay[bb][sub - 1:sub, pl_]
                s_ref[idx] = new
                sb_ref[idx] = new.astype(BF16)
        for bb in batches:
            y = u0_ref[bb, sub:2 * sub, :]
            for s in range(sub):
                y = y + jnp.where(
                    step >= s, dots(bb, s, 2) * us[bb][s] + dots(bb, s, 3) * vs[bb][s], 0.0)
            y_ref[bb, pl.ds(t0, sub), :] = y

    def pack_steps(i, carry):
        t_pack = pl.multiple_of(i * pack, pack)
        for half in range(pack // sub):
            solve_tile(t_pack, half)
        return carry

    lax.fori_loop(0, tc // pack, pack_steps, 0)


def _wkv(r, w, k, v, na, b, batch):
    n, c = r.shape
    seq = n // batch
    tc = WKV_CHUNK
    n_pairs = c // LANES
    shaped = [a.reshape(batch, seq, c) for a in (r, w, k, v, na, b)]
    spec = pl.BlockSpec((batch, tc, c), lambda i: (0, i, 0))
    y = pl.pallas_call(
        _wkv_body,
        grid=(seq // tc,),
        in_specs=[spec] * 6,
        out_specs=spec,
        out_shape=jax.ShapeDtypeStruct((batch, seq, c), F32),
        scratch_shapes=[pltpu.VMEM((batch * n_pairs, LANES, LANES), F32),
                        pltpu.VMEM((batch * n_pairs, LANES, LANES), BF16),
                        pltpu.VMEM((batch, 4 * WKV_TILE * WKV_TILE, c), BF16),
                        pltpu.VMEM((batch, 4 * WKV_TILE * WKV_TILE, c), F32),
                        pltpu.VMEM((batch, 2 * WKV_TILE, c), F32)],
        compiler_params=_params("arbitrary"),
        name="wkv7_scan",
    )(*shaped)
    return y.reshape(n, c)


def _rwkv_out_body(y_ref, r_ref, k_ref, v_ref, g_ref, h_ref, lnw_ref, lnb_ref, rk_ref, wo_ref,
                   gffn_ref, router_ref, out_ref, hnp_ref, route_ref, z_ref):
    j = pl.program_id(1)
    tn = wo_ref.shape[1]

    @pl.when(j == 0)
    def _():
        ones = _head_ones()
        y = y_ref[...]
        mean = _head_sum(y, ones) * (1.0 / HEAD_DIM)
        yc = y - mean
        var = _head_sum(yc * yc, ones) * (1.0 / HEAD_DIM)
        yn = yc * lax.rsqrt(var + GN_EPS) * lnw_ref[...] + lnb_ref[...]
        rk = r_ref[...].astype(F32) * k_ref[...].astype(F32) * rk_ref[...]
        bonus = _head_sum(rk, ones) * v_ref[...].astype(F32)
        z_ref[...] = ((yn + bonus) * g_ref[...].astype(F32)).astype(BF16)

    col = pl.multiple_of(j * tn, tn)
    out_ref[:, pl.ds(col, tn)] = h_ref[:, pl.ds(col, tn)] + jnp.dot(
        z_ref[...], wo_ref[...], preferred_element_type=F32)

    @pl.when(j == pl.num_programs(1) - 1)
    def _():
        hn = _rms_norm(out_ref[...], gffn_ref[...])
        half = hn.shape[1] // 2
        lo_bits = lax.bitcast_convert_type(hn[:, :half].astype(BF16).astype(F32), U32) >> 16
        hi_bits = lax.bitcast_convert_type(hn[:, half:].astype(BF16).astype(F32), U32)
        hnp_ref[...] = (hi_bits & jnp.uint32(0xFFFF0000)) | lo_bits
        logits = jnp.dot(hn, router_ref[...], preferred_element_type=F32,
                         precision=lax.Precision.HIGHEST)
        lane = lax.broadcasted_iota(I32, logits.shape, 1)
        neg = jnp.float32(-jnp.inf)
        lg = jnp.where(lane < N_EXPERTS, logits, neg)
        m1 = jnp.max(lg, axis=1, keepdims=True)
        i1 = jnp.min(jnp.where(lg == m1, lane, LANES), axis=1, keepdims=True)
        lg2 = jnp.where(lane == i1, neg, lg)
        m2 = jnp.max(lg2, axis=1, keepdims=True)
        i2 = jnp.min(jnp.where(lg2 == m2, lane, LANES), axis=1, keepdims=True)
        e = jnp.exp(m2 - m1)
        g1 = 1.0 / (1.0 + e)
        g2 = e / (1.0 + e)
        route_ref[...] = jnp.where(
            lane == 0, i1.astype(F32),
            jnp.where(lane == 1, i2.astype(F32),
                      jnp.where(lane == 2, g1, jnp.where(lane == 3, g2, 0.0))))


def _rwkv_out(y, r, k, v, g, h, lnw, lnb, rk, wo, g_ffn, router):
    n, c = h.shape
    tm, tn = OUT_ROWS, COL_TILE
    row = pl.BlockSpec((tm, c), lambda i, j: (i, 0))
    vec = pl.BlockSpec((1, c), lambda i, j: (0, 0))
    return pl.pallas_call(
        _rwkv_out_body,
        grid=(n // tm, c // tn),
        in_specs=[row, row, row, row, row, row, vec, vec, vec,
                  pl.BlockSpec((c, tn), lambda i, j: (0, j)), vec,
                  pl.BlockSpec(router.shape, lambda i, j: (0, 0))],
        out_specs=[row, pl.BlockSpec((tm, c // 2), lambda i, j: (i, 0)),
                   pl.BlockSpec((tm, LANES), lambda i, j: (i, 0))],
        out_shape=[jax.ShapeDtypeStruct((n, c), F32), jax.ShapeDtypeStruct((n, c // 2), U32),
                   jax.ShapeDtypeStruct((n, LANES), F32)],
        scratch_shapes=[pltpu.VMEM((tm, c), BF16)],
        compiler_params=_params("parallel", "arbitrary"),
        name="rwkv_out_router",
    )(y, r, k, v, g, h, lnw, lnb, rk, wo, g_ffn, router)


def _route_slots(route, rows_per_block):
    n = route.shape[0]
    experts = route[:, :TOP_K].astype(I32)
    flat = experts.reshape(-1)
    onehot = (flat[None, :] == jnp.arange(N_EXPERTS, dtype=I32)[:, None]).astype(I32)
    csum = jnp.cumsum(onehot, axis=1)
    counts = csum[:, -1]
    padded = (counts + rows_per_block - 1) // rows_per_block * rows_per_block
    pad_end = jnp.cumsum(padded)
    pad_start = pad_end - padded
    slot = jnp.sum(onehot * (csum - 1 + pad_start[:, None]), axis=0).reshape(n, TOP_K)
    n_blocks = -(-(n * TOP_K) // rows_per_block) + N_EXPERTS
    blk_start = jnp.arange(n_blocks, dtype=I32) * rows_per_block
    blk_expert = jnp.minimum(jnp.sum((blk_start[:, None] >= pad_end[None, :]).astype(I32), axis=1),
                             N_EXPERTS - 1)
    n_used = pad_end[-1] // rows_per_block
    return slot, blk_expert.astype(I32), n_used.astype(I32).reshape(1), n_blocks


def _dispatch_body(s0_ref, s1_ref, x_ref, xs_in_ref, xs_ref, sem):
    del xs_in_ref
    rows = x_ref.shape[0]

    def copy(r, slot):
        return pltpu.make_async_copy(x_ref.at[pl.ds(r, 1)], xs_ref.at[pl.ds(slot, 1)], sem)

    def issue(r, carry):
        copy(r, s0_ref[0, 0, r]).start()
        copy(r, s1_ref[0, 0, r]).start()
        return carry

    lax.fori_loop(0, rows, issue, 0, unroll=DMA_UNROLL)

    def drain(r, carry):
        copy(0, 0).wait()
        copy(0, 0).wait()
        return carry

    lax.fori_loop(0, rows, drain, 0, unroll=DMA_UNROLL)


def _dispatch(xp, slot, n_slots):
    n, w = xp.shape
    tr = ROUTE_ROWS
    s0 = slot[:, 0].reshape(n // tr, 1, tr)
    s1 = slot[:, 1].reshape(n // tr, 1, tr)
    smem = pl.BlockSpec((1, 1, tr), lambda i: (i, 0, 0), memory_space=pltpu.SMEM)
    return pl.pallas_call(
        _dispatch_body,
        grid=(n // tr,),
        in_specs=[smem, smem, pl.BlockSpec((tr, w), lambda i: (i, 0)),
                  pl.BlockSpec(memory_space=pl.ANY)],
        out_specs=pl.BlockSpec(memory_space=pl.ANY),
        out_shape=jax.ShapeDtypeStruct((n_slots, w), xp.dtype),
        scratch_shapes=[pltpu.SemaphoreType.DMA(())],
        input_output_aliases={3: 0},
        compiler_params=_params("arbitrary"),
        name="moe_dispatch",
    )(s0, s1, xp, jnp.zeros((n_slots, w), xp.dtype))


def _moe_body(be_ref, nu_ref, x_ref, wg_ref, wu_ref, wd_ref, y_ref, xb_ref):
    del be_ref
    b = pl.program_id(0)
    f = pl.program_id(1)

    @pl.when(f == 0)
    def _():
        y_ref[...] = jnp.zeros_like(y_ref)
        packed = x_ref[...]
        half = packed.shape[1]
        xb_ref[:, :half] = lax.bitcast_convert_type(packed << 16, F32).astype(BF16)
        xb_ref[:, half:] = lax.bitcast_convert_type(
            packed & jnp.uint32(0xFFFF0000), F32).astype(BF16)

    @pl.when(b < nu_ref[0])
    def _():
        x = xb_ref[...]
        gate = jnp.dot(x, wg_ref[...].astype(BF16), preferred_element_type=F32)
        up = jnp.dot(x, wu_ref[...].astype(BF16), preferred_element_type=F32)
        act = (gate * _sigmoid(gate) * up).astype(BF16)
        y_ref[...] += jnp.dot(act, wd_ref[...].astype(BF16), preferred_element_type=F32)


def _moe_experts(xs, blk_expert, n_used, wg, wu, wd, layer, n_blocks):
    n_slots, half = xs.shape
    c = 2 * half
    ff = wg.shape[3]
    tb, tf = MOE_ROWS, MOE_FF_TILE
    nf = ff // tf

    def used(b, nu):
        return jnp.minimum(b, nu[0] - 1)

    def f_of(b, f, nu):
        return jnp.where(b < nu[0], f, nf - 1)

    grid_spec = pltpu.PrefetchScalarGridSpec(
        num_scalar_prefetch=2,
        grid=(n_blocks, nf),
        in_specs=[
            pl.BlockSpec((tb, half), lambda b, f, be, nu: (used(b, nu), 0)),
            pl.BlockSpec((None, None, c, tf),
                         lambda b, f, be, nu: (layer, be[used(b, nu)], 0, f_of(b, f, nu))),
            pl.BlockSpec((None, None, c, tf),
                         lambda b, f, be, nu: (layer, be[used(b, nu)], 0, f_of(b, f, nu))),
            pl.BlockSpec((None, None, tf, c),
                         lambda b, f, be, nu: (layer, be[used(b, nu)], f_of(b, f, nu), 0)),
        ],
        out_specs=pl.BlockSpec((tb, c), lambda b, f, be, nu: (b, 0)),
        scratch_shapes=[pltpu.VMEM((tb, c), BF16)],
    )
    return pl.pallas_call(
        _moe_body,
        grid_spec=grid_spec,
        out_shape=jax.ShapeDtypeStruct((n_slots, c), F32),
        compiler_params=_params("arbitrary", "arbitrary"),
        name="moe_experts",
    )(blk_expert, n_used, xs, wg, wu, wd)


def _combine_body(s0_ref, s1_ref, route_ref, h_ref, g_ref, ys_ref, out_ref, buf_ref, sem, *,
                  final_norm):
    rows = h_ref.shape[0]

    def copy(r, slot, k):
        return pltpu.make_async_copy(ys_ref.at[pl.ds(slot, 1)], buf_ref.at[k, pl.ds(r, 1)], sem)

    def issue(r, carry):
        copy(r, s0_ref[0, 0, r], 0).start()
        copy(r, s1_ref[0, 0, r], 1).start()
        return carry

    lax.fori_loop(0, rows, issue, 0, unroll=DMA_UNROLL)

    def drain(r, carry):
        copy(0, 0, 0).wait()
        copy(0, 0, 1).wait()
        return carry

    lax.fori_loop(0, rows, drain, 0, unroll=DMA_UNROLL)
    route = route_ref[...]
    out = h_ref[...] + buf_ref[0] * route[:, 2:3] + buf_ref[1] * route[:, 3:4]
    if final_norm:
        out = _rms_norm(out, g_ref[...])
    out_ref[...] = out


def _combine(ys, slot, route, h, g_final, final_norm):
    n, c = h.shape
    tr = ROUTE_ROWS
    s0 = slot[:, 0].reshape(n // tr, 1, tr)
    s1 = slot[:, 1].reshape(n // tr, 1, tr)
    smem = pl.BlockSpec((1, 1, tr), lambda i: (i, 0, 0), memory_space=pltpu.SMEM)
    return pl.pallas_call(
        functools.partial(_combine_body, final_norm=final_norm),
        grid=(n // tr,),
        in_specs=[smem, smem, pl.BlockSpec((tr, LANES), lambda i: (i, 0)),
                  pl.BlockSpec((tr, c), lambda i: (i, 0)), pl.BlockSpec((1, c), lambda i: (0, 0)),
                  pl.BlockSpec(memory_space=pl.ANY)],
        out_specs=pl.BlockSpec((tr, c), lambda i: (i, 0)),
        out_shape=jax.ShapeDtypeStruct((n, c), F32),
        scratch_shapes=[pltpu.VMEM((TOP_K, tr, c), F32), pltpu.SemaphoreType.DMA(())],
        compiler_params=_params("arbitrary"),
        name="moe_combine_norm" if final_norm else "moe_combine",
    )(s0, s1, route, h, g_final, ys)


def _cast_body(w_ref, o_ref):
    o_ref[...] = w_ref[...].astype(o_ref.dtype)


def _layer_bf16(w, layer):
    cols = w.shape[-1]
    w3 = w.reshape(w.shape[0], -1, cols)
    rows = w3.shape[1]
    blk = 1 << ((CAST_BLOCK_BYTES // (cols * 4)).bit_length() - 1)
    assert rows % blk == 0, (rows, blk)
    out = pl.pallas_call(
        _cast_body,
        grid=(rows // blk,),
        in_specs=[pl.BlockSpec((None, blk, cols), lambda i: (layer, i, 0))],
        out_specs=pl.BlockSpec((blk, cols), lambda i: (i, 0)),
        out_shape=jax.ShapeDtypeStruct((rows, cols), BF16),
        compiler_params=_params("parallel"),
        name="cast_bf16",
    )(w3)
    return out.reshape(w.shape[1:])


def _pad_cols(w, mult=LANES):
    pad = -w.shape[1] % mult
    return jnp.pad(w, ((0, 0), (0, pad))) if pad else w


def _pad_rows(w, mult=LANES):
    pad = -w.shape[0] % mult
    return jnp.pad(w, ((0, pad), (0, 0))) if pad else w


def kernel(x, norm_mix, norm_ffn, norm_final, pool_w, pool_scale, rwkv_mu, rwkv_w0, rwkv_w1, rwkv_w2, rwkv_a0, rwkv_a1, rwkv_a2, rwkv_v0, rwkv_v1, rwkv_v2, rwkv_g1, rwkv_g2, rwkv_k_k, rwkv_k_a, rwkv_r_k, rwkv_wr, rwkv_wk, rwkv_wv, rwkv_wo, rwkv_lnx_w, rwkv_lnx_b, ffn_w_gate, ffn_w_up, ffn_w_down, moe_router, moe_w_gate, moe_w_up, moe_w_down):
    batch, seq, c = x.shape
    n = batch * seq
    depth = norm_mix.shape[0]
    vec = lambda a: a.reshape(1, c)
    h = x.reshape(n, c)
    hn = None
    v_first = None
    for i in range(depth):
        j = i // 2
        if i % 2 == 0:
            h, hn_ffn = _pool_layer(h, seq, vec(norm_mix[i]), pool_w[j].astype(BF16),
                                    vec(pool_scale[j]), vec(norm_ffn[i]))
            h, hn = _ffn_layer(hn_ffn, h, _layer_bf16(ffn_w_gate, j), _layer_bf16(ffn_w_up, j),
                               _layer_bf16(ffn_w_down, j), vec(norm_mix[i + 1]))
        else:
            p = {
                "mu": jnp.pad(rwkv_mu[j], ((0, 2), (0, 0))),
                "wr": _layer_bf16(rwkv_wr, j), "wk": _layer_bf16(rwkv_wk, j),
                "wv": _layer_bf16(rwkv_wv, j),
                "w1": _pad_cols(rwkv_w1[j]).astype(BF16), "w2": _pad_rows(rwkv_w2[j]).astype(BF16),
                "a1": _pad_cols(rwkv_a1[j]).astype(BF16), "a2": _pad_rows(rwkv_a2[j]).astype(BF16),
                "g1": _pad_cols(rwkv_g1[j]).astype(BF16), "g2": _pad_rows(rwkv_g2[j]).astype(BF16),
                "w0": vec(rwkv_w0[j]), "a0": vec(rwkv_a0[j]),
                "k_k": vec(rwkv_k_k[j]), "k_a": vec(rwkv_k_a[j]),
            }
            if j > 0:
                p["v1"] = _pad_cols(rwkv_v1[j - 1]).astype(BF16)
                p["v2"] = _pad_rows(rwkv_v2[j - 1]).astype(BF16)
                p["v0"] = vec(rwkv_v0[j - 1])
            r, d, k, v, na, b, g = _rwkv_proj(hn, seq, p, v_first if j > 0 else None)
            if v_first is None:
                v_first = v
            y = _wkv(r, d, k, v, na, b, batch)
            h, hn_packed, route = _rwkv_out(
                y, r, k, v, g, h, vec(rwkv_lnx_w[j]), vec(rwkv_lnx_b[j]), vec(rwkv_r_k[j]),
                _layer_bf16(rwkv_wo, j), vec(norm_ffn[i]), _pad_cols(moe_router[j]))
            slot, blk_expert, n_used, n_blocks = _route_slots(route, MOE_ROWS)
            xs = _dispatch(hn_packed, slot, n_blocks * MOE_ROWS)
            ys = _moe_experts(xs, blk_expert, n_used, moe_w_gate, moe_w_up, moe_w_down, j, n_blocks)
            h = _combine(ys, slot, route, h, vec(norm_final), final_norm=(i == depth - 1))
    return h.reshape(batch, seq, c)
```

```python
import functools

import jax
import jax.numpy as jnp
from jax import lax
from jax.experimental import pallas as pl
from jax.experimental.pallas import tpu as pltpu

F32 = jnp.float32
BF16 = jnp.bfloat16
I32 = jnp.int32
U32 = jnp.uint32

LANES = 128
HEAD_DIM = 64
HEADS_PER_VREG = LANES // HEAD_DIM
RMS_EPS = 1e-6
GN_EPS = 64e-5
POOL_WINDOWS = (2, 4, 8, 16)
POOL_HALO = 16
N_EXPERTS = 8
TOP_K = 2
VMEM_LIMIT_BYTES = 56 * 1024 * 1024

ROW_TILE = 512
PROJ_ROWS = 512
OUT_ROWS = 256
COL_TILE = 512
FF_TILE = 512
MOE_FF_TILE = 512
MOE_ROWS = 1024
ROUTE_ROWS = 256
DMA_UNROLL = 8
CAST_BLOCK_BYTES = 8 * 1024 * 1024
WKV_CHUNK = 64
WKV_LANES = 256
WKV_TILE = 8


def _params(*sem):
    return pltpu.CompilerParams(dimension_semantics=sem, vmem_limit_bytes=VMEM_LIMIT_BYTES)


def _rms_norm(x, g):
    return x * lax.rsqrt(jnp.mean(x * x, axis=-1, keepdims=True) + RMS_EPS) * g


def _sigmoid(x):
    return 1.0 / (1.0 + jnp.exp(-x))


def _head_ones():
    r = lax.broadcasted_iota(I32, (LANES, LANES), 0) // HEAD_DIM
    c = lax.broadcasted_iota(I32, (LANES, LANES), 1) // HEAD_DIM
    return (r == c).astype(BF16)


def _head_sum(x, ones):
    outs = []
    for c in range(x.shape[1] // LANES):
        xc = x[:, c * LANES:(c + 1) * LANES]
        hi = xc.astype(BF16)
        lo = (xc - hi.astype(F32)).astype(BF16)
        outs.append(jnp.dot(hi, ones, preferred_element_type=F32)
                    + jnp.dot(lo, ones, preferred_element_type=F32))
    return outs[0] if len(outs) == 1 else jnp.concatenate(outs, axis=1)


def _pool_body(x_ref, halo_ref, gmix_ref, w_ref, scale_ref, gffn_ref, h_ref, hn_ref, *,
               tiles_per_seq):
    tm = x_ref.shape[0]
    group = w_ref.shape[1]
    tile_in_seq = pl.program_id(0) % tiles_per_seq
    x = x_ref[...]
    g = gmix_ref[...]
    xn = _rms_norm(x, g)
    keep = (tile_in_seq > 0).astype(F32)
    cat = jnp.concatenate([_rms_norm(halo_ref[...], g) * keep, xn], axis=0)
    row = lax.broadcasted_iota(I32, (tm, 1), 0)
    pos = (tile_in_seq * tm + row + 1).astype(F32)
    for gi, win in enumerate(POOL_WINDOWS):
        lo = gi * group
        s = cat[:, lo:lo + group]
        k = 1
        while k < win:
            s = s + pltpu.roll(s, k, 0)
            k *= 2
        pooled = s[POOL_HALO:, :] / jnp.minimum(pos, float(win))
        d = (pooled - xn[:, lo:lo + group]).astype(BF16)
        mixed = jnp.dot(d, w_ref[gi], preferred_element_type=F32)
        h_ref[:, lo:lo + group] = x[:, lo:lo + group] + mixed * scale_ref[:, lo:lo + group]
    hn_ref[...] = _rms_norm(h_ref[...], gffn_ref[...]).astype(BF16)


def _pool_layer(h, seq, g_mix, w, scale, g_ffn):
    n, c = h.shape
    tm = ROW_TILE
    halo_blocks = tm // POOL_HALO
    return pl.pallas_call(
        functools.partial(_pool_body, tiles_per_seq=seq // tm),
        grid=(n // tm,),
        in_specs=[
            pl.BlockSpec((tm, c), lambda i: (i, 0)),
            pl.BlockSpec((POOL_HALO, c), lambda i: (jnp.maximum(i * halo_blocks - 1, 0), 0)),
            pl.BlockSpec((1, c), lambda i: (0, 0)),
            pl.BlockSpec(w.shape, lambda i: (0, 0, 0)),
            pl.BlockSpec((1, c), lambda i: (0, 0)),
            pl.BlockSpec((1, c), lambda i: (0, 0)),
        ],
        out_specs=[pl.BlockSpec((tm, c), lambda i: (i, 0)), pl.BlockSpec((tm, c), lambda i: (i, 0))],
        out_shape=[jax.ShapeDtypeStruct((n, c), F32), jax.ShapeDtypeStruct((n, c), BF16)],
        compiler_params=_params("parallel"),
        name="pool_mixer",
    )(h, h, g_mix, w, scale, g_ffn)


def _ffn_body(x_ref, wg_ref, wu_ref, wd_ref, h_ref, gnext_ref, out_ref, hn_ref):
    f = pl.program_id(1)

    @pl.when(f == 0)
    def _():
        out_ref[...] = h_ref[...]

    x = x_ref[...]
    gate = jnp.dot(x, wg_ref[...], preferred_element_type=F32)
    up = jnp.dot(x, wu_ref[...], preferred_element_type=F32)
    act = (gate * _sigmoid(gate) * up).astype(BF16)
    out_ref[...] += jnp.dot(act, wd_ref[...], preferred_element_type=F32)

    @pl.when(f == pl.num_programs(1) - 1)
    def _():
        hn_ref[...] = _rms_norm(out_ref[...], gnext_ref[...])


def _ffn_layer(hn, h, wg, wu, wd, g_next):
    n, c = h.shape
    ff = wg.shape[1]
    tm, tf = ROW_TILE, FF_TILE
    return pl.pallas_call(
        _ffn_body,
        grid=(n // tm, ff // tf),
        in_specs=[
            pl.BlockSpec((tm, c), lambda i, f: (i, 0)),
            pl.BlockSpec((c, tf), lambda i, f: (0, f)),
            pl.BlockSpec((c, tf), lambda i, f: (0, f)),
            pl.BlockSpec((tf, c), lambda i, f: (f, 0)),
            pl.BlockSpec((tm, c), lambda i, f: (i, 0)),
            pl.BlockSpec((1, c), lambda i, f: (0, 0)),
        ],
        out_specs=[pl.BlockSpec((tm, c), lambda i, f: (i, 0)),
                   pl.BlockSpec((tm, c), lambda i, f: (i, 0))],
        out_shape=[jax.ShapeDtypeStruct((n, c), F32), jax.ShapeDtypeStruct((n, c), F32)],
        compiler_params=_params("parallel", "arbitrary"),
        name="dense_swiglu",
    )(hn, wg, wu, wd, h, g_next)


def _rwkv_proj_body(*refs, tiles_per_seq, has_vres):
    if has_vres:
        (hn_ref, halo_ref, mu_ref, wr_ref, wk_ref, wv_ref, w1_ref, a1_ref, g1_ref,
         w2_ref, a2_ref, g2_ref, w0_ref, a0_ref, kk_ref, ka_ref,
         v1_ref, v2_ref, v0_ref, vf_ref,
         r_out, d_out, k_out, v_out, na_out, b_out, g_out,
         xs_ref, hw_ref, ha_ref, hg_ref, hv_ref) = refs
    else:
        (hn_ref, halo_ref, mu_ref, wr_ref, wk_ref, wv_ref, w1_ref, a1_ref, g1_ref,
         w2_ref, a2_ref, g2_ref, w0_ref, a0_ref, kk_ref, ka_ref,
         r_out, d_out, k_out, v_out, na_out, b_out, g_out,
         xs_ref, hw_ref, ha_ref, hg_ref) = refs
    tm = hn_ref.shape[0]

    @pl.when(pl.program_id(1) == 0)
    def _():
        hn = hn_ref[...]
        keep = (pl.program_id(0) % tiles_per_seq > 0).astype(F32)
        before = halo_ref[halo_ref.shape[0] - 1:, :] * keep
        row = lax.broadcasted_iota(I32, (tm, 1), 0)
        prev = jnp.where(row == 0, before, pltpu.roll(hn, 1, 0))
        xx = prev - hn
        for m in range(6):
            xs_ref[m] = (hn + xx * mu_ref[m:m + 1, :]).astype(BF16)
        hw_ref[...] = jnp.tanh(jnp.dot(xs_ref[1], w1_ref[...], preferred_element_type=F32)).astype(BF16)
        ha_ref[...] = jnp.dot(xs_ref[4], a1_ref[...], preferred_element_type=F32).astype(BF16)
        hg_ref[...] = _sigmoid(jnp.dot(xs_ref[5], g1_ref[...], preferred_element_type=F32)).astype(BF16)
        if has_vres:
            hv_ref[...] = jnp.dot(xs_ref[3], v1_ref[...], preferred_element_type=F32).astype(BF16)

    wl = jnp.dot(hw_ref[...], w2_ref[...], preferred_element_type=F32)
    z = -(w0_ref[...] + wl)
    softplus = jnp.maximum(z, 0.0) + jnp.log(1.0 + jnp.exp(-jnp.abs(z)))
    d_out[...] = -jnp.exp(-softplus - 0.5)
    al = jnp.dot(ha_ref[...], a2_ref[...], preferred_element_type=F32)
    a = _sigmoid(a0_ref[...] + al)
    k = jnp.dot(xs_ref[2], wk_ref[...], preferred_element_type=F32)
    kk = k * kk_ref[...]
    norm = jnp.sqrt(_head_sum(kk * kk, _head_ones()))
    kk = kk / jnp.maximum(norm, 1e-12)
    k_out[...] = (k * (1.0 + (a - 1.0) * ka_ref[...])).astype(k_out.dtype)
    na_out[...] = (-kk).astype(na_out.dtype)
    b_out[...] = (kk * a).astype(b_out.dtype)
    v = jnp.dot(xs_ref[3], wv_ref[...], preferred_element_type=F32)
    if has_vres:
        vl = jnp.dot(hv_ref[...], v2_ref[...], preferred_element_type=F32)
        v = v + (vf_ref[...].astype(F32) - v) * _sigmoid(v0_ref[...] + vl)
    v_out[...] = v.astype(v_out.dtype)
    gl = jnp.dot(hg_ref[...], g2_ref[...], preferred_element_type=F32)
    g_out[...] = gl.astype(g_out.dtype)
    r = jnp.dot(xs_ref[0], wr_ref[...], preferred_element_type=F32)
    r_out[...] = r.astype(r_out.dtype)


def _rwkv_proj(hn, seq, p, v_first):
    n, c = hn.shape
    tm, tn = PROJ_ROWS, COL_TILE
    has_vres = v_first is not None
    halo_rows = 8
    row_spec = pl.BlockSpec((tm, c), lambda i, j: (i, 0))
    halo_spec = pl.BlockSpec((halo_rows, c),
                             lambda i, j: (jnp.maximum(i * (tm // halo_rows) - 1, 0), 0))
    full = lambda a: pl.BlockSpec(a.shape, lambda i, j: (0,) * a.ndim)
    cols = lambda a: pl.BlockSpec((a.shape[0], tn), lambda i, j: (0, j))
    tile = pl.BlockSpec((tm, tn), lambda i, j: (i, j))
    args = [hn, hn, p["mu"], p["wr"], p["wk"], p["wv"], p["w1"], p["a1"], p["g1"],
            p["w2"], p["a2"], p["g2"], p["w0"], p["a0"], p["k_k"], p["k_a"]]
    specs = [row_spec, halo_spec, full(p["mu"]), cols(p["wr"]), cols(p["wk"]), cols(p["wv"]),
             full(p["w1"]), full(p["a1"]), full(p["g1"]),
             cols(p["w2"]), cols(p["a2"]), cols(p["g2"]),
             cols(p["w0"]), cols(p["a0"]), cols(p["k_k"]), cols(p["k_a"])]
    scratch = [pltpu.VMEM((6, tm, c), BF16), pltpu.VMEM((tm, p["w1"].shape[1]), BF16),
               pltpu.VMEM((tm, p["a1"].shape[1]), BF16), pltpu.VMEM((tm, p["g1"].shape[1]), BF16)]
    if has_vres:
        args += [p["v1"], p["v2"], p["v0"], v_first]
        specs += [full(p["v1"]), cols(p["v2"]), cols(p["v0"]), tile]
        scratch.append(pltpu.VMEM((tm, p["v1"].shape[1]), BF16))
    return pl.pallas_call(
        functools.partial(_rwkv_proj_body, tiles_per_seq=seq // tm, has_vres=has_vres),
        grid=(n // tm, c // tn),
        in_specs=specs,
        out_specs=[tile] * 7,
        out_shape=[jax.ShapeDtypeStruct((n, c), F32 if i == 1 else BF16) for i in range(7)],
        scratch_shapes=scratch,
        compiler_params=_params("parallel", "arbitrary"),
        name="rwkv_proj_vres" if has_vres else "rwkv_proj",
    )(*args)


def _wkv_body(r_ref, w_ref, k_ref, v_ref, na_ref, b_ref, y_ref, s_ref, sb_ref, lhs_ref, e_ref, u0_ref):
    nb, tc, c = r_ref.shape
    sub = WKV_TILE
    pack = 2 * sub
    n_pairs = c // LANES
    n_groups = c // WKV_LANES
    kinds = 4

    @pl.when(pl.program_id(0) == 0)
    def _():
        s_ref[...] = jnp.zeros_like(s_ref)
        sb_ref[...] = jnp.zeros_like(sb_ref)

    head_r = lax.broadcasted_iota(I32, (WKV_LANES, WKV_LANES), 0) // HEAD_DIM
    head_c = lax.broadcasted_iota(I32, (WKV_LANES, WKV_LANES), 1) // HEAD_DIM
    ones = (head_r == head_c).astype(BF16)
    same_head = (lax.broadcasted_iota(I32, (LANES, LANES), 0) // HEAD_DIM
                 == lax.broadcasted_iota(I32, (LANES, LANES), 1) // HEAD_DIM)
    step = lax.broadcasted_iota(I32, (sub, c), 0)

    def shift_down(x, d):
        return jnp.where(step >= d, pltpu.roll(x, d, 0), 0.0)

    def spread(x, s):
        return jnp.broadcast_to(x[s:s + 1, :], (sub, c))

    def dots(bb, s, kind):
        start = (s * kinds + kind) * sub
        return e_ref[bb, start:start + sub, :]

    def solve_tile(t_pack, half):
        t0 = pl.multiple_of(t_pack + half * sub, sub)
        rows = slice(half * sub, (half + 1) * sub)
        batches = range(nb)
        decay, at, rt, bt, kt, v, vs, u, us = ([None] * nb for _ in range(9))
        for bb in batches:
            load = lambda ref: ref[bb, pl.ds(t_pack, pack), :].astype(F32)[rows, :]
            w = w_ref[bb, pl.ds(t0, sub), :]
            cum = w
            for d in (1, 2, 4):
                cum = cum + shift_down(cum, d)
            grow = jnp.exp(-cum)
            decay[bb] = jnp.exp(cum)
            at[bb] = load(na_ref) * jnp.exp(cum - w)
            rt[bb] = load(r_ref) * decay[bb]
            bt[bb] = load(b_ref) * grow
            kt[bb] = load(k_ref) * grow
            v[bb] = load(v_ref)
            for s in range(sub):
                bs, ks = spread(bt[bb], s), spread(kt[bb], s)
                base = s * kinds * sub
                lhs_ref[bb, base:base + 2 * sub, :] = jnp.concatenate(
                    [at[bb] * bs, at[bb] * ks], axis=0).astype(BF16)
                lhs_ref[bb, base + 2 * sub:base + 4 * sub, :] = jnp.concatenate(
                    [rt[bb] * bs, rt[bb] * ks], axis=0).astype(BF16)
            for g in range(n_groups):
                gl = slice(g * WKV_LANES, (g + 1) * WKV_LANES)
                e_ref[bb, :, gl] = jnp.dot(lhs_ref[bb, :, gl], ones, preferred_element_type=F32)
        for bb in batches:
            ar = jnp.concatenate([at[bb], rt[bb]], axis=0).astype(BF16)
            for p in range(n_pairs):
                pl_ = slice(p * LANES, (p + 1) * LANES)
                u0_ref[bb, :, pl_] = lax.dot_general(
                    ar[:, pl_], sb_ref[bb * n_pairs + p], (((1,), (1,)), ((), ())),
                    preferred_element_type=F32)
        for bb in batches:
            vs[bb] = [spread(v[bb], s) for s in range(sub)]
            u[bb] = u0_ref[bb, 0:sub, :]
            us[bb] = []
            for s in range(sub):
                us[bb].append(spread(u[bb], s))
                if s < sub - 1:
                    u[bb] = u[bb] + jnp.where(
                        step > s, dots(bb, s, 0) * us[bb][s] + dots(bb, s, 1) * vs[bb][s], 0.0)
        for bb in batches:
            uv = jnp.concatenate([u[bb], v[bb]], axis=0).astype(BF16)
            bk = jnp.concatenate([bt[bb], kt[bb]], axis=0).astype(BF16)
            for p in range(n_pairs):
                pl_ = slice(p * LANES, (p + 1) * LANES)
                idx = bb * n_pairs + p
                upd = lax.dot_general(uv[:, pl_], bk[:, pl_], (((0,), (0,)), ((), ())),
                                      preferred_element_type=F32)
                new = jnp.where(same_head, s_ref[idx] + upd, 0.0) * decay[bb][sub - 1:sub, pl_]
                s_ref[idx] = new
                sb_ref[idx] = new.astype(BF16)
        for bb in batches:
            y = u0_ref[bb, sub:2 * sub, :]
            for s in range(sub):
                y = y + jnp.where(
                    step >= s, dots(bb, s, 2) * us[bb][s] + dots(bb, s, 3) * vs[bb][s], 0.0)
            y_ref[bb, pl.ds(t0, sub), :] = y

    def pack_steps(i, carry):
        t_pack = pl.multiple_of(i * pack, pack)
        for half in range(pack // sub):
            solve_tile(t_pack, half)
        return carry

    lax.fori_loop(0, tc // pack, pack_steps, 0)


def _wkv(r, w, k, v, na, b, batch):
    n, c = r.shape
    seq = n // batch
    tc = WKV_CHUNK
    n_pairs = c // LANES
    shaped = [a.reshape(batch, seq, c) for a in (r, w, k, v, na, b)]
    spec = pl.BlockSpec((batch, tc, c), lambda i: (0, i, 0))
    y = pl.pallas_call(
        _wkv_body,
        grid=(seq // tc,),
        in_specs=[spec] * 6,
        out_specs=spec,
        out_shape=jax.ShapeDtypeStruct((batch, seq, c), F32),
        scratch_shapes=[pltpu.VMEM((batch * n_pairs, LANES, LANES), F32),
                        pltpu.VMEM((batch * n_pairs, LANES, LANES), BF16),
                        pltpu.VMEM((batch, 4 * WKV_TILE * WKV_TILE, c), BF16),
                        pltpu.VMEM((batch, 4 * WKV_TILE * WKV_TILE, c), F32),
                        pltpu.VMEM((batch, 2 * WKV_TILE, c), F32)],
        compiler_params=_params("arbitrary"),
        name="wkv7_scan",
    )(*shaped)
    return y.reshape(n, c)


def _rwkv_out_body(y_ref, r_ref, k_ref, v_ref, g_ref, h_ref, lnw_ref, lnb_ref, rk_ref, wo_ref,
                   gffn_ref, router_ref, out_ref, hnp_ref, route_ref, z_ref):
    j = pl.program_id(1)
    tn = wo_ref.shape[1]

    @pl.when(j == 0)
    def _():
        ones = _head_ones()
        y = y_ref[...]
        mean = _head_sum(y, ones) * (1.0 / HEAD_DIM)
        yc = y - mean
        var = _head_sum(yc * yc, ones) * (1.0 / HEAD_DIM)
        yn = yc * lax.rsqrt(var + GN_EPS) * lnw_ref[...] + lnb_ref[...]
        rk = r_ref[...].astype(F32) * k_ref[...].astype(F32) * rk_ref[...]
        bonus = _head_sum(rk, ones) * v_ref[...].astype(F32)
        z_ref[...] = ((yn + bonus) * g_ref[...].astype(F32)).astype(BF16)

    col = pl.multiple_of(j * tn, tn)
    out_ref[:, pl.ds(col, tn)] = h_ref[:, pl.ds(col, tn)] + jnp.dot(
        z_ref[...], wo_ref[...], preferred_element_type=F32)

    @pl.when(j == pl.num_programs(1) - 1)
    def _():
        hn = _rms_norm(out_ref[...], gffn_ref[...])
        half = hn.shape[1] // 2
        lo_bits = lax.bitcast_convert_type(hn[:, :half].astype(BF16).astype(F32), U32) >> 16
        hi_bits = lax.bitcast_convert_type(hn[:, half:].astype(BF16).astype(F32), U32)
        hnp_ref[...] = (hi_bits & jnp.uint32(0xFFFF0000)) | lo_bits
        logits = jnp.dot(hn, router_ref[...], preferred_element_type=F32,
                         precision=lax.Precision.HIGHEST)
        lane = lax.broadcasted_iota(I32, logits.shape, 1)
        neg = jnp.float32(-jnp.inf)
        lg = jnp.where(lane < N_EXPERTS, logits, neg)
        m1 = jnp.max(lg, axis=1, keepdims=True)
        i1 = jnp.min(jnp.where(lg == m1, lane, LANES), axis=1, keepdims=True)
        lg2 = jnp.where(lane == i1, neg, lg)
        m2 = jnp.max(lg2, axis=1, keepdims=True)
        i2 = jnp.min(jnp.where(lg2 == m2, lane, LANES), axis=1, keepdims=True)
        e = jnp.exp(m2 - m1)
        g1 = 1.0 / (1.0 + e)
        g2 = e / (1.0 + e)
        route_ref[...] = jnp.where(
            lane == 0, i1.astype(F32),
            jnp.where(lane == 1, i2.astype(F32),
                      jnp.where(lane == 2, g1, jnp.where(lane == 3, g2, 0.0))))


def _rwkv_out(y, r, k, v, g, h, lnw, lnb, rk, wo, g_ffn, router):
    n, c = h.shape
    tm, tn = OUT_ROWS, COL_TILE
    row = pl.BlockSpec((tm, c), lambda i, j: (i, 0))
    vec = pl.BlockSpec((1, c), lambda i, j: (0, 0))
    return pl.pallas_call(
        _rwkv_out_body,
        grid=(n // tm, c // tn),
        in_specs=[row, row, row, row, row, row, vec, vec, vec,
                  pl.BlockSpec((c, tn), lambda i, j: (0, j)), vec,
                  pl.BlockSpec(router.shape, lambda i, j: (0, 0))],
        out_specs=[row, pl.BlockSpec((tm, c // 2), lambda i, j: (i, 0)),
                   pl.BlockSpec((tm, LANES), lambda i, j: (i, 0))],
        out_shape=[jax.ShapeDtypeStruct((n, c), F32), jax.ShapeDtypeStruct((n, c // 2), U32),
                   jax.ShapeDtypeStruct((n, LANES), F32)],
        scratch_shapes=[pltpu.VMEM((tm, c), BF16)],
        compiler_params=_params("parallel", "arbitrary"),
        name="rwkv_out_router",
    )(y, r, k, v, g, h, lnw, lnb, rk, wo, g_ffn, router)


def _route_slots(route, rows_per_block):
    n = route.shape[0]
    experts = route[:, :TOP_K].astype(I32)
    flat = experts.reshape(-1)
    onehot = (flat[None, :] == jnp.arange(N_EXPERTS, dtype=I32)[:, None]).astype(I32)
    csum = jnp.cumsum(onehot, axis=1)
    counts = csum[:, -1]
    padded = (counts + rows_per_block - 1) // rows_per_block * rows_per_block
    pad_end = jnp.cumsum(padded)
    pad_start = pad_end - padded
    slot = jnp.sum(onehot * (csum - 1 + pad_start[:, None]), axis=0).reshape(n, TOP_K)
    n_blocks = -(-(n * TOP_K) // rows_per_block) + N_EXPERTS
    blk_start = jnp.arange(n_blocks, dtype=I32) * rows_per_block
    blk_expert = jnp.minimum(jnp.sum((blk_start[:, None] >= pad_end[None, :]).astype(I32), axis=1),
                             N_EXPERTS - 1)
    n_used = pad_end[-1] // rows_per_block
    return slot, blk_expert.astype(I32), n_used.astype(I32).reshape(1), n_blocks


def _dispatch_body(s0_ref, s1_ref, x_ref, xs_in_ref, xs_ref, sem):
    del xs_in_ref
    rows = x_ref.shape[0]

    def copy(r, slot):
        return pltpu.make_async_copy(x_ref.at[pl.ds(r, 1)], xs_ref.at[pl.ds(slot, 1)], sem)

    def issue(r, carry):
        copy(r, s0_ref[0, 0, r]).start()
        copy(r, s1_ref[0, 0, r]).start()
        return carry

    lax.fori_loop(0, rows, issue, 0, unroll=DMA_UNROLL)

    def drain(r, carry):
        copy(0, 0).wait()
        copy(0, 0).wait()
        return carry

    lax.fori_loop(0, rows, drain, 0, unroll=DMA_UNROLL)


def _dispatch(xp, slot, n_slots):
    n, w = xp.shape
    tr = ROUTE_ROWS
    s0 = slot[:, 0].reshape(n // tr, 1, tr)
    s1 = slot[:, 1].reshape(n // tr, 1, tr)
    smem = pl.BlockSpec((1, 1, tr), lambda i: (i, 0, 0), memory_space=pltpu.SMEM)
    return pl.pallas_call(
        _dispatch_body,
        grid=(n // tr,),
        in_specs=[smem, smem, pl.BlockSpec((tr, w), lambda i: (i, 0)),
                  pl.BlockSpec(memory_space=pl.ANY)],
        out_specs=pl.BlockSpec(memory_space=pl.ANY),
        out_shape=jax.ShapeDtypeStruct((n_slots, w), xp.dtype),
        scratch_shapes=[pltpu.SemaphoreType.DMA(())],
        input_output_aliases={3: 0},
        compiler_params=_params("arbitrary"),
        name="moe_dispatch",
    )(s0, s1, xp, jnp.zeros((n_slots, w), xp.dtype))


def _moe_body(be_ref, nu_ref, x_ref, wg_ref, wu_ref, wd_ref, y_ref, xb_ref):
    del be_ref
    b = pl.program_id(0)
    f = pl.program_id(1)

    @pl.when(f == 0)
    def _():
        y_ref[...] = jnp.zeros_like(y_ref)
        packed = x_ref[...]
        half = packed.shape[1]
        xb_ref[:, :half] = lax.bitcast_convert_type(packed << 16, F32).astype(BF16)
        xb_ref[:, half:] = lax.bitcast_convert_type(
            packed & jnp.uint32(0xFFFF0000), F32).astype(BF16)

    @pl.when(b < nu_ref[0])
    def _():
        x = xb_ref[...]
        gate = jnp.dot(x, wg_ref[...].astype(BF16), preferred_element_type=F32)
        up = jnp.dot(x, wu_ref[...].astype(BF16), preferred_element_type=F32)
        act = (gate * _sigmoid(gate) * up).astype(BF16)
        y_ref[...] += jnp.dot(act, wd_ref[...].astype(BF16), preferred_element_type=F32)


def _moe_experts(xs, blk_expert, n_used, wg, wu, wd, layer, n_blocks):
    n_slots, half = xs.shape
    c = 2 * half
    ff = wg.shape[3]
    tb, tf = MOE_ROWS, MOE_FF_TILE
    nf = ff // tf

    def used(b, nu):
        return jnp.minimum(b, nu[0] - 1)

    def f_of(b, f, nu):
        return jnp.where(b < nu[0], f, nf - 1)

    grid_spec = pltpu.PrefetchScalarGridSpec(
        num_scalar_prefetch=2,
        grid=(n_blocks, nf),
        in_specs=[
            pl.BlockSpec((tb, half), lambda b, f, be, nu: (used(b, nu), 0),
                         pipeline_mode=pl.Buffered(1)),
            pl.BlockSpec((None, None, c, tf),
                         lambda b, f, be, nu: (layer, be[used(b, nu)], 0, f_of(b, f, nu))),
            pl.BlockSpec((None, None, c, tf),
                         lambda b, f, be, nu: (layer, be[used(b, nu)], 0, f_of(b, f, nu))),
            pl.BlockSpec((None, None, tf, c),
                         lambda b, f, be, nu: (layer, be[used(b, nu)], f_of(b, f, nu), 0)),
        ],
        out_specs=pl.BlockSpec((tb, c), lambda b, f, be, nu: (b, 0), pipeline_mode=pl.Buffered(1)),
        scratch_shapes=[pltpu.VMEM((tb, c), BF16)],
    )
    return pl.pallas_call(
        _moe_body,
        grid_spec=grid_spec,
        out_shape=jax.ShapeDtypeStruct((n_slots, c), F32),
        compiler_params=_params("arbitrary", "arbitrary"),
        name="moe_experts",
    )(blk_expert, n_used, xs, wg, wu, wd)


def _combine_body(s0_ref, s1_ref, route_ref, h_ref, g_ref, ys_ref, out_ref, buf_ref, sem, *,
                  final_norm):
    rows = h_ref.shape[0]

    def copy(r, slot, k):
        return pltpu.make_async_copy(ys_ref.at[pl.ds(slot, 1)], buf_ref.at[k, pl.ds(r, 1)], sem)

    def issue(r, carry):
        copy(r, s0_ref[0, 0, r], 0).start()
        copy(r, s1_ref[0, 0, r], 1).start()
        return carry

    lax.fori_loop(0, rows, issue, 0, unroll=DMA_UNROLL)

    def drain(r, carry):
        copy(0, 0, 0).wait()
        copy(0, 0, 1).wait()
        return carry

    lax.fori_loop(0, rows, drain, 0, unroll=DMA_UNROLL)
    route = route_ref[...]
    out = h_ref[...] + buf_ref[0] * route[:, 2:3] + buf_ref[1] * route[:, 3:4]
    if final_norm:
        out = _rms_norm(out, g_ref[...])
    out_ref[...] = out


def _combine(ys, slot, route, h, g_final, final_norm):
    n, c = h.shape
    tr = ROUTE_ROWS
    s0 = slot[:, 0].reshape(n // tr, 1, tr)
    s1 = slot[:, 1].reshape(n // tr, 1, tr)
    smem = pl.BlockSpec((1, 1, tr), lambda i: (i, 0, 0), memory_space=pltpu.SMEM)
    return pl.pallas_call(
        functools.partial(_combine_body, final_norm=final_norm),
        grid=(n // tr,),
        in_specs=[smem, smem, pl.BlockSpec((tr, LANES), lambda i: (i, 0)),
                  pl.BlockSpec((tr, c), lambda i: (i, 0)), pl.BlockSpec((1, c), lambda i: (0, 0)),
                  pl.BlockSpec(memory_space=pl.ANY)],
        out_specs=pl.BlockSpec((tr, c), lambda i: (i, 0)),
        out_shape=jax.ShapeDtypeStruct((n, c), F32),
        scratch_shapes=[pltpu.VMEM((TOP_K, tr, c), F32), pltpu.SemaphoreType.DMA(())],
        compiler_params=_params("arbitrary"),
        name="moe_combine_norm" if final_norm else "moe_combine",
    )(s0, s1, route, h, g_final, ys)


def _cast_body(w_ref, o_ref):
    o_ref[...] = w_ref[...].astype(o_ref.dtype)


def _layer_bf16(w, layer):
    cols = w.shape[-1]
    w3 = w.reshape(w.shape[0], -1, cols)
    rows = w3.shape[1]
    blk = 1 << ((CAST_BLOCK_BYTES // (cols * 4)).bit_length() - 1)
    assert rows % blk == 0, (rows, blk)
    out = pl.pallas_call(
        _cast_body,
        grid=(rows // blk,),
        in_specs=[pl.BlockSpec((None, blk, cols), lambda i: (layer, i, 0))],
        out_specs=pl.BlockSpec((blk, cols), lambda i: (i, 0)),
        out_shape=jax.ShapeDtypeStruct((rows, cols), BF16),
        compiler_params=_params("parallel"),
        name="cast_bf16",
    )(w3)
    return out.reshape(w.shape[1:])


def _pad_cols(w, mult=LANES):
    pad = -w.shape[1] % mult
    return jnp.pad(w, ((0, 0), (0, pad))) if pad else w


def _pad_rows(w, mult=LANES):
    pad = -w.shape[0] % mult
    return jnp.pad(w, ((0, pad), (0, 0))) if pad else w


def kernel(x, norm_mix, norm_ffn, norm_final, pool_w, pool_scale, rwkv_mu, rwkv_w0, rwkv_w1, rwkv_w2, rwkv_a0, rwkv_a1, rwkv_a2, rwkv_v0, rwkv_v1, rwkv_v2, rwkv_g1, rwkv_g2, rwkv_k_k, rwkv_k_a, rwkv_r_k, rwkv_wr, rwkv_wk, rwkv_wv, rwkv_wo, rwkv_lnx_w, rwkv_lnx_b, ffn_w_gate, ffn_w_up, ffn_w_down, moe_router, moe_w_gate, moe_w_up, moe_w_down):
    batch, seq, c = x.shape
    n = batch * seq
    depth = norm_mix.shape[0]
    vec = lambda a: a.reshape(1, c)
    h = x.reshape(n, c)
    hn = None
    v_first = None
    for i in range(depth):
        j = i // 2
        if i % 2 == 0:
            h, hn_ffn = _pool_layer(h, seq, vec(norm_mix[i]), pool_w[j].astype(BF16),
                                    vec(pool_scale[j]), vec(norm_ffn[i]))
            h, hn = _ffn_layer(hn_ffn, h, _layer_bf16(ffn_w_gate, j), _layer_bf16(ffn_w_up, j),
                               _layer_bf16(ffn_w_down, j), vec(norm_mix[i + 1]))
        else:
            p = {
                "mu": jnp.pad(rwkv_mu[j], ((0, 2), (0, 0))),
                "wr": _layer_bf16(rwkv_wr, j), "wk": _layer_bf16(rwkv_wk, j),
                "wv": _layer_bf16(rwkv_wv, j),
                "w1": _pad_cols(rwkv_w1[j]).astype(BF16), "w2": _pad_rows(rwkv_w2[j]).astype(BF16),
                "a1": _pad_cols(rwkv_a1[j]).astype(BF16), "a2": _pad_rows(rwkv_a2[j]).astype(BF16),
                "g1": _pad_cols(rwkv_g1[j]).astype(BF16), "g2": _pad_rows(rwkv_g2[j]).astype(BF16),
                "w0": vec(rwkv_w0[j]), "a0": vec(rwkv_a0[j]),
                "k_k": vec(rwkv_k_k[j]), "k_a": vec(rwkv_k_a[j]),
            }
            if j > 0:
                p["v1"] = _pad_cols(rwkv_v1[j - 1]).astype(BF16)
                p["v2"] = _pad_rows(rwkv_v2[j - 1]).astype(BF16)
                p["v0"] = vec(rwkv_v0[j - 1])
            r, d, k, v, na, b, g = _rwkv_proj(hn, seq, p, v_first if j > 0 else None)
            if v_first is None:
                v_first = v
            y = _wkv(r, d, k, v, na, b, batch)
            h, hn_packed, route = _rwkv_out(
                y, r, k, v, g, h, vec(rwkv_lnx_w[j]), vec(rwkv_lnx_b[j]), vec(rwkv_r_k[j]),
                _layer_bf16(rwkv_wo, j), vec(norm_ffn[i]), _pad_cols(moe_router[j]))
            slot, blk_expert, n_used, n_blocks = _route_slots(route, MOE_ROWS)
            xs = _dispatch(hn_packed, slot, n_blocks * MOE_ROWS)
            ys = _moe_experts(xs, blk_expert, n_used, moe_w_gate, moe_w_up, moe_w_down, j, n_blocks)
            h = _combine(ys, slot, route, h, vec(norm_final), final_norm=(i == depth - 1))
    return h.reshape(batch, seq, c)
```

```python
import functools

import jax
import jax.numpy as jnp
from jax import lax
from jax.experimental import pallas as pl
from jax.experimental.pallas import tpu as pltpu

F32 = jnp.float32
BF16 = jnp.bfloat16
I32 = jnp.int32
U32 = jnp.uint32

LANES = 128
HEAD_DIM = 64
HEADS_PER_VREG = LANES // HEAD_DIM
RMS_EPS = 1e-6
GN_EPS = 64e-5
POOL_WINDOWS = (2, 4, 8, 16)
POOL_HALO = 16
N_EXPERTS = 8
TOP_K = 2
VMEM_LIMIT_BYTES = 56 * 1024 * 1024

ROW_TILE = 512
PROJ_ROWS = 512
OUT_ROWS = 256
COL_TILE = 512
FF_TILE = 1024
MOE_FF_TILE = 512
MOE_ROWS = 768
ROUTE_ROWS = 256
DMA_UNROLL = 8
CAST_BLOCK_BYTES = 8 * 1024 * 1024
WKV_CHUNK = 128
WKV_LANES = 256
WKV_TILE = 8


def _params(*sem):
    return pltpu.CompilerParams(dimension_semantics=sem, vmem_limit_bytes=VMEM_LIMIT_BYTES)


def _rms_norm(x, g):
    return x * lax.rsqrt(jnp.mean(x * x, axis=-1, keepdims=True) + RMS_EPS) * g


def _sigmoid(x):
    return 1.0 / (1.0 + jnp.exp(-x))


def _head_ones():
    r = lax.broadcasted_iota(I32, (LANES, LANES), 0) // HEAD_DIM
    c = lax.broadcasted_iota(I32, (LANES, LANES), 1) // HEAD_DIM
    return (r == c).astype(BF16)


def _head_sum(x, ones):
    outs = []
    for c in range(x.shape[1] // LANES):
        xc = x[:, c * LANES:(c + 1) * LANES]
        hi = xc.astype(BF16)
        lo = (xc - hi.astype(F32)).astype(BF16)
        outs.append(jnp.dot(hi, ones, preferred_element_type=F32)
                    + jnp.dot(lo, ones, preferred_element_type=F32))
    return outs[0] if len(outs) == 1 else jnp.concatenate(outs, axis=1)


def _pool_body(x_ref, halo_ref, gmix_ref, w_ref, scale_ref, gffn_ref, h_ref, hn_ref, *,
               tiles_per_seq):
    tm = x_ref.shape[0]
    group = w_ref.shape[1]
    tile_in_seq = pl.program_id(0) % tiles_per_seq
    x = x_ref[...]
    g = gmix_ref[...]
    xn = _rms_norm(x, g)
    keep = (tile_in_seq > 0).astype(F32)
    cat = jnp.concatenate([_rms_norm(halo_ref[...], g) * keep, xn], axis=0)
    row = lax.broadcasted_iota(I32, (tm, 1), 0)
    pos = (tile_in_seq * tm + row + 1).astype(F32)
    for gi, win in enumerate(POOL_WINDOWS):
        lo = gi * group
        s = cat[:, lo:lo + group]
        k = 1
        while k < win:
            s = s + pltpu.roll(s, k, 0)
            k *= 2
        pooled = s[POOL_HALO:, :] / jnp.minimum(pos, float(win))
        d = (pooled - xn[:, lo:lo + group]).astype(BF16)
        mixed = jnp.dot(d, w_ref[gi], preferred_element_type=F32)
        h_ref[:, lo:lo + group] = x[:, lo:lo + group] + mixed * scale_ref[:, lo:lo + group]
    hn_ref[...] = _rms_norm(h_ref[...], gffn_ref[...]).astype(BF16)


def _pool_layer(h, seq, g_mix, w, scale, g_ffn):
    n, c = h.shape
    tm = ROW_TILE
    halo_blocks = tm // POOL_HALO
    return pl.pallas_call(
        functools.partial(_pool_body, tiles_per_seq=seq // tm),
        grid=(n // tm,),
        in_specs=[
            pl.BlockSpec((tm, c), lambda i: (i, 0)),
            pl.BlockSpec((POOL_HALO, c), lambda i: (jnp.maximum(i * halo_blocks - 1, 0), 0)),
            pl.BlockSpec((1, c), lambda i: (0, 0)),
            pl.BlockSpec(w.shape, lambda i: (0, 0, 0)),
            pl.BlockSpec((1, c), lambda i: (0, 0)),
            pl.BlockSpec((1, c), lambda i: (0, 0)),
        ],
        out_specs=[pl.BlockSpec((tm, c), lambda i: (i, 0)), pl.BlockSpec((tm, c), lambda i: (i, 0))],
        out_shape=[jax.ShapeDtypeStruct((n, c), F32), jax.ShapeDtypeStruct((n, c), BF16)],
        compiler_params=_params("parallel"),
        name="pool_mixer",
    )(h, h, g_mix, w, scale, g_ffn)


def _ffn_body(x_ref, wg_ref, wu_ref, wd_ref, h_ref, gnext_ref, out_ref, hn_ref):
    f = pl.program_id(1)

    @pl.when(f == 0)
    def _():
        out_ref[...] = h_ref[...]

    x = x_ref[...]
    gate = jnp.dot(x, wg_ref[...], preferred_element_type=F32)
    up = jnp.dot(x, wu_ref[...], preferred_element_type=F32)
    act = (gate * _sigmoid(gate) * up).astype(BF16)
    out_ref[...] += jnp.dot(act, wd_ref[...], preferred_element_type=F32)

    @pl.when(f == pl.num_programs(1) - 1)
    def _():
        hn_ref[...] = _rms_norm(out_ref[...], gnext_ref[...])


def _ffn_layer(hn, h, wg, wu, wd, g_next):
    n, c = h.shape
    ff = wg.shape[1]
    tm, tf = ROW_TILE, FF_TILE
    return pl.pallas_call(
        _ffn_body,
        grid=(n // tm, ff // tf),
        in_specs=[
            pl.BlockSpec((tm, c), lambda i, f: (i, 0)),
            pl.BlockSpec((c, tf), lambda i, f: (0, f)),
            pl.BlockSpec((c, tf), lambda i, f: (0, f)),
            pl.BlockSpec((tf, c), lambda i, f: (f, 0)),
            pl.BlockSpec((tm, c), lambda i, f: (i, 0), pipeline_mode=pl.Buffered(1)),
            pl.BlockSpec((1, c), lambda i, f: (0, 0)),
        ],
        out_specs=[pl.BlockSpec((tm, c), lambda i, f: (i, 0)),
                   pl.BlockSpec((tm, c), lambda i, f: (i, 0))],
        out_shape=[jax.ShapeDtypeStruct((n, c), F32), jax.ShapeDtypeStruct((n, c), F32)],
        compiler_params=_params("parallel", "arbitrary"),
        name="dense_swiglu",
    )(hn, wg, wu, wd, h, g_next)


def _rwkv_proj_body(*refs, tiles_per_seq, has_vres):
    if has_vres:
        (hn_ref, halo_ref, mu_ref, wr_ref, wk_ref, wv_ref, w1_ref, a1_ref, g1_ref,
         w2_ref, a2_ref, g2_ref, w0_ref, a0_ref, kk_ref, ka_ref,
         v1_ref, v2_ref, v0_ref, vf_ref,
         r_out, d_out, k_out, v_out, na_out, b_out, g_out,
         xs_ref, hw_ref, ha_ref, hg_ref, hv_ref) = refs
    else:
        (hn_ref, halo_ref, mu_ref, wr_ref, wk_ref, wv_ref, w1_ref, a1_ref, g1_ref,
         w2_ref, a2_ref, g2_ref, w0_ref, a0_ref, kk_ref, ka_ref,
         r_out, d_out, k_out, v_out, na_out, b_out, g_out,
         xs_ref, hw_ref, ha_ref, hg_ref) = refs
    tm = hn_ref.shape[0]

    @pl.when(pl.program_id(1) == 0)
    def _():
        hn = hn_ref[...]
        keep = (pl.program_id(0) % tiles_per_seq > 0).astype(F32)
        before = halo_ref[halo_ref.shape[0] - 1:, :] * keep
        row = lax.broadcasted_iota(I32, (tm, 1), 0)
        prev = jnp.where(row == 0, before, pltpu.roll(hn, 1, 0))
        xx = prev - hn
        for m in range(6):
            xs_ref[m] = (hn + xx * mu_ref[m:m + 1, :]).astype(BF16)
        hw_ref[...] = jnp.tanh(jnp.dot(xs_ref[1], w1_ref[...], preferred_element_type=F32)).astype(BF16)
        ha_ref[...] = jnp.dot(xs_ref[4], a1_ref[...], preferred_element_type=F32).astype(BF16)
        hg_ref[...] = _sigmoid(jnp.dot(xs_ref[5], g1_ref[...], preferred_element_type=F32)).astype(BF16)
        if has_vres:
            hv_ref[...] = jnp.dot(xs_ref[3], v1_ref[...], preferred_element_type=F32).astype(BF16)

    wl = jnp.dot(hw_ref[...], w2_ref[...], preferred_element_type=F32)
    z = -(w0_ref[...] + wl)
    softplus = jnp.maximum(z, 0.0) + jnp.log(1.0 + jnp.exp(-jnp.abs(z)))
    d_out[...] = -jnp.exp(-softplus - 0.5)
    al = jnp.dot(ha_ref[...], a2_ref[...], preferred_element_type=F32)
    a = _sigmoid(a0_ref[...] + al)
    k = jnp.dot(xs_ref[2], wk_ref[...], preferred_element_type=F32)
    kk = k * kk_ref[...]
    norm = jnp.sqrt(_head_sum(kk * kk, _head_ones()))
    kk = kk / jnp.maximum(norm, 1e-12)
    k_out[...] = (k * (1.0 + (a - 1.0) * ka_ref[...])).astype(k_out.dtype)
    na_out[...] = (-kk).astype(na_out.dtype)
    b_out[...] = (kk * a).astype(b_out.dtype)
    v = jnp.dot(xs_ref[3], wv_ref[...], preferred_element_type=F32)
    if has_vres:
        vl = jnp.dot(hv_ref[...], v2_ref[...], preferred_element_type=F32)
        v = v + (vf_ref[...].astype(F32) - v) * _sigmoid(v0_ref[...] + vl)
    v_out[...] = v.astype(v_out.dtype)
    gl = jnp.dot(hg_ref[...], g2_ref[...], preferred_element_type=F32)
    g_out[...] = gl.astype(g_out.dtype)
    r = jnp.dot(xs_ref[0], wr_ref[...], preferred_element_type=F32)
    r_out[...] = r.astype(r_out.dtype)


def _rwkv_proj(hn, seq, p, v_first):
    n, c = hn.shape
    tm, tn = PROJ_ROWS, COL_TILE
    has_vres = v_first is not None
    halo_rows = 8
    row_spec = pl.BlockSpec((tm, c), lambda i, j: (i, 0))
    halo_spec = pl.BlockSpec((halo_rows, c),
                             lambda i, j: (jnp.maximum(i * (tm // halo_rows) - 1, 0), 0))
    full = lambda a: pl.BlockSpec(a.shape, lambda i, j: (0,) * a.ndim)
    cols = lambda a: pl.BlockSpec((a.shape[0], tn), lambda i, j: (0, j))
    tile = pl.BlockSpec((tm, tn), lambda i, j: (i, j))
    args = [hn, hn, p["mu"], p["wr"], p["wk"], p["wv"], p["w1"], p["a1"], p["g1"],
            p["w2"], p["a2"], p["g2"], p["w0"], p["a0"], p["k_k"], p["k_a"]]
    specs = [row_spec, halo_spec, full(p["mu"]), cols(p["wr"]), cols(p["wk"]), cols(p["wv"]),
             full(p["w1"]), full(p["a1"]), full(p["g1"]),
             cols(p["w2"]), cols(p["a2"]), cols(p["g2"]),
             cols(p["w0"]), cols(p["a0"]), cols(p["k_k"]), cols(p["k_a"])]
    scratch = [pltpu.VMEM((6, tm, c), BF16), pltpu.VMEM((tm, p["w1"].shape[1]), BF16),
               pltpu.VMEM((tm, p["a1"].shape[1]), BF16), pltpu.VMEM((tm, p["g1"].shape[1]), BF16)]
    if has_vres:
        args += [p["v1"], p["v2"], p["v0"], v_first]
        specs += [full(p["v1"]), cols(p["v2"]), cols(p["v0"]), tile]
        scratch.append(pltpu.VMEM((tm, p["v1"].shape[1]), BF16))
    return pl.pallas_call(
        functools.partial(_rwkv_proj_body, tiles_per_seq=seq // tm, has_vres=has_vres),
        grid=(n // tm, c // tn),
        in_specs=specs,
        out_specs=[tile] * 7,
        out_shape=[jax.ShapeDtypeStruct((n, c), F32 if i == 1 else BF16) for i in range(7)],
        scratch_shapes=scratch,
        compiler_params=_params("parallel", "arbitrary"),
        name="rwkv_proj_vres" if has_vres else "rwkv_proj",
    )(*args)


def _wkv_body(r_ref, w_ref, k_ref, v_ref, na_ref, b_ref, y_ref, s_ref, sb_ref, lhs_ref, e_ref, u0_ref):
    nb, tc, c = r_ref.shape
    sub = WKV_TILE
    pack = 2 * sub
    n_pairs = c // LANES
    n_groups = c // WKV_LANES
    kinds = 4

    @pl.when(pl.program_id(0) == 0)
    def _():
        s_ref[...] = jnp.zeros_like(s_ref)
        sb_ref[...] = jnp.zeros_like(sb_ref)

    head_r = lax.broadcasted_iota(I32, (WKV_LANES, WKV_LANES), 0) // HEAD_DIM
    head_c = lax.broadcasted_iota(I32, (WKV_LANES, WKV_LANES), 1) // HEAD_DIM
    ones = (head_r == head_c).astype(BF16)
    same_head = (lax.broadcasted_iota(I32, (LANES, LANES), 0) // HEAD_DIM
                 == lax.broadcasted_iota(I32, (LANES, LANES), 1) // HEAD_DIM)
    step = lax.broadcasted_iota(I32, (sub, c), 0)

    def shift_down(x, d):
        return jnp.where(step >= d, pltpu.roll(x, d, 0), 0.0)

    def spread(x, s):
        return jnp.broadcast_to(x[s:s + 1, :], (sub, c))

    def dots(bb, s, kind):
        start = (s * kinds + kind) * sub
        return e_ref[bb, start:start + sub, :]

    def solve_tile(t_pack, half):
        t0 = pl.multiple_of(t_pack + half * sub, sub)
        rows = slice(half * sub, (half + 1) * sub)
        batches = range(nb)
        decay, at, rt, bt, kt, v, vs, u, us = ([None] * nb for _ in range(9))
        for bb in batches:
            load = lambda ref: ref[bb, pl.ds(t_pack, pack), :].astype(F32)[rows, :]
            w = w_ref[bb, pl.ds(t0, sub), :]
            cum = w
            for d in (1, 2, 4):
                cum = cum + shift_down(cum, d)
            grow = jnp.exp(-cum)
            decay[bb] = jnp.exp(cum)
            at[bb] = load(na_ref) * jnp.exp(cum - w)
            rt[bb] = load(r_ref) * decay[bb]
            bt[bb] = load(b_ref) * grow
            kt[bb] = load(k_ref) * grow
            v[bb] = load(v_ref)
            for s in range(sub):
                bs, ks = spread(bt[bb], s), spread(kt[bb], s)
                base = s * kinds * sub
                lhs_ref[bb, base:base + 2 * sub, :] = jnp.concatenate(
                    [at[bb] * bs, at[bb] * ks], axis=0).astype(BF16)
                lhs_ref[bb, base + 2 * sub:base + 4 * sub, :] = jnp.concatenate(
                    [rt[bb] * bs, rt[bb] * ks], axis=0).astype(BF16)
            for g in range(n_groups):
                gl = slice(g * WKV_LANES, (g + 1) * WKV_LANES)
                e_ref[bb, :, gl] = jnp.dot(lhs_ref[bb, :, gl], ones, preferred_element_type=F32)
        for bb in batches:
            ar = jnp.concatenate([at[bb], rt[bb]], axis=0).astype(BF16)
            for p in range(n_pairs):
                pl_ = slice(p * LANES, (p + 1) * LANES)
                u0_ref[bb, :, pl_] = lax.dot_general(
                    ar[:, pl_], sb_ref[bb * n_pairs + p], (((1,), (1,)), ((), ())),
                    preferred_element_type=F32)
        for bb in batches:
            vs[bb] = [spread(v[bb], s) for s in range(sub)]
            u[bb] = u0_ref[bb, 0:sub, :]
            us[bb] = []
            for s in range(sub):
                us[bb].append(spread(u[bb], s))
                if s < sub - 1:
                    u[bb] = u[bb] + jnp.where(
                        step > s, dots(bb, s, 0) * us[bb][s] + dots(bb, s, 1) * vs[bb][s], 0.0)
        for bb in batches:
            uv = jnp.concatenate([u[bb], v[bb]], axis=0).astype(BF16)
            bk = jnp.concatenate([bt[bb], kt[bb]], axis=0).astype(BF16)
            for p in range(n_pairs):
                pl_ = slice(p * LANES, (p + 1) * LANES)
                idx = bb * n_pairs + p
                upd = lax.dot_general(uv[:, pl_], bk[:, pl_], (((0,), (0,)), ((), ())),
                                      preferred_element_type=F32)
                new = jnp.where(same_head, s_ref[idx] + upd, 0.0) * decay[bb][sub - 1:sub, pl_]
                s_ref[idx] = new
                sb_ref[idx] = new.astype(BF16)
        for bb in batches:
            y = u0_ref[bb, sub:2 * sub, :]
            for s in range(sub):
                y = y + jnp.where(
                    step >= s, dots(bb, s, 2) * us[bb][s] + dots(bb, s, 3) * vs[bb][s], 0.0)
            y_ref[bb, pl.ds(t0, sub), :] = y

    def pack_steps(i, carry):
        t_pack = pl.multiple_of(i * pack, pack)
        for half in range(pack // sub):
            solve_tile(t_pack, half)
        return carry

    lax.fori_loop(0, tc // pack, pack_steps, 0)


def _wkv(r, w, k, v, na, b, batch):
    n, c = r.shape
    seq = n // batch
    tc = WKV_CHUNK
    n_pairs = c // LANES
    shaped = [a.reshape(batch, seq, c) for a in (r, w, k, v, na, b)]
    spec = pl.BlockSpec((batch, tc, c), lambda i: (0, i, 0))
    y = pl.pallas_call(
        _wkv_body,
        grid=(seq // tc,),
        in_specs=[spec] * 6,
        out_specs=spec,
        out_shape=jax.ShapeDtypeStruct((batch, seq, c), F32),
        scratch_shapes=[pltpu.VMEM((batch * n_pairs, LANES, LANES), F32),
                        pltpu.VMEM((batch * n_pairs, LANES, LANES), BF16),
                        pltpu.VMEM((batch, 4 * WKV_TILE * WKV_TILE, c), BF16),
                        pltpu.VMEM((batch, 4 * WKV_TILE * WKV_TILE, c), F32),
                        pltpu.VMEM((batch, 2 * WKV_TILE, c), F32)],
        compiler_params=_params("arbitrary"),
        name="wkv7_scan",
    )(*shaped)
    return y.reshape(n, c)


def _rwkv_out_body(y_ref, r_ref, k_ref, v_ref, g_ref, h_ref, lnw_ref, lnb_ref, rk_ref, wo_ref,
                   gffn_ref, router_ref, out_ref, hnp_ref, route_ref, z_ref):
    j = pl.program_id(1)
    tn = wo_ref.shape[1]

    @pl.when(j == 0)
    def _():
        ones = _head_ones()
        y = y_ref[...]
        mean = _head_sum(y, ones) * (1.0 / HEAD_DIM)
        yc = y - mean
        var = _head_sum(yc * yc, ones) * (1.0 / HEAD_DIM)
        yn = yc * lax.rsqrt(var + GN_EPS) * lnw_ref[...] + lnb_ref[...]
        rk = r_ref[...].astype(F32) * k_ref[...].astype(F32) * rk_ref[...]
        bonus = _head_sum(rk, ones) * v_ref[...].astype(F32)
        z_ref[...] = ((yn + bonus) * g_ref[...].astype(F32)).astype(BF16)

    col = pl.multiple_of(j * tn, tn)
    out_ref[:, pl.ds(col, tn)] = h_ref[:, pl.ds(col, tn)] + jnp.dot(
        z_ref[...], wo_ref[...], preferred_element_type=F32)

    @pl.when(j == pl.num_programs(1) - 1)
    def _():
        hn = _rms_norm(out_ref[...], gffn_ref[...])
        half = hn.shape[1] // 2
        lo_bits = lax.bitcast_convert_type(hn[:, :half].astype(BF16).astype(F32), U32) >> 16
        hi_bits = lax.bitcast_convert_type(hn[:, half:].astype(BF16).astype(F32), U32)
        hnp_ref[...] = (hi_bits & jnp.uint32(0xFFFF0000)) | lo_bits
        logits = jnp.dot(hn, router_ref[...], preferred_element_type=F32,
                         precision=lax.Precision.HIGHEST)
        lane = lax.broadcasted_iota(I32, logits.shape, 1)
        neg = jnp.float32(-jnp.inf)
        lg = jnp.where(lane < N_EXPERTS, logits, neg)
        m1 = jnp.max(lg, axis=1, keepdims=True)
        i1 = jnp.min(jnp.where(lg == m1, lane, LANES), axis=1, keepdims=True)
        lg2 = jnp.where(lane == i1, neg, lg)
        m2 = jnp.max(lg2, axis=1, keepdims=True)
        i2 = jnp.min(jnp.where(lg2 == m2, lane, LANES), axis=1, keepdims=True)
        e = jnp.exp(m2 - m1)
        g1 = 1.0 / (1.0 + e)
        g2 = e / (1.0 + e)
        route_ref[...] = jnp.where(
            lane == 0, i1.astype(F32),
            jnp.where(lane == 1, i2.astype(F32),
                      jnp.where(lane == 2, g1, jnp.where(lane == 3, g2, 0.0))))


def _rwkv_out(y, r, k, v, g, h, lnw, lnb, rk, wo, g_ffn, router):
    n, c = h.shape
    tm, tn = OUT_ROWS, COL_TILE
    row = pl.BlockSpec((tm, c), lambda i, j: (i, 0))
    vec = pl.BlockSpec((1, c), lambda i, j: (0, 0))
    return pl.pallas_call(
        _rwkv_out_body,
        grid=(n // tm, c // tn),
        in_specs=[row, row, row, row, row, row, vec, vec, vec,
                  pl.BlockSpec((c, tn), lambda i, j: (0, j)), vec,
                  pl.BlockSpec(router.shape, lambda i, j: (0, 0))],
        out_specs=[row, pl.BlockSpec((tm, c // 2), lambda i, j: (i, 0)),
                   pl.BlockSpec((tm, LANES), lambda i, j: (i, 0))],
        out_shape=[jax.ShapeDtypeStruct((n, c), F32), jax.ShapeDtypeStruct((n, c // 2), U32),
                   jax.ShapeDtypeStruct((n, LANES), F32)],
        scratch_shapes=[pltpu.VMEM((tm, c), BF16)],
        compiler_params=_params("parallel", "arbitrary"),
        name="rwkv_out_router",
    )(y, r, k, v, g, h, lnw, lnb, rk, wo, g_ffn, router)


def _route_slots(route, rows_per_block):
    n = route.shape[0]
    experts = route[:, :TOP_K].astype(I32)
    flat = experts.reshape(-1)
    onehot = (flat[None, :] == jnp.arange(N_EXPERTS, dtype=I32)[:, None]).astype(I32)
    csum = jnp.cumsum(onehot, axis=1)
    counts = csum[:, -1]
    padded = (counts + rows_per_block - 1) // rows_per_block * rows_per_block
    pad_end = jnp.cumsum(padded)
    pad_start = pad_end - padded
    slot = jnp.sum(onehot * (csum - 1 + pad_start[:, None]), axis=0).reshape(n, TOP_K)
    n_blocks = -(-(n * TOP_K) // rows_per_block) + N_EXPERTS
    blk_start = jnp.arange(n_blocks, dtype=I32) * rows_per_block
    blk_expert = jnp.minimum(jnp.sum((blk_start[:, None] >= pad_end[None, :]).astype(I32), axis=1),
                             N_EXPERTS - 1)
    n_used = pad_end[-1] // rows_per_block
    return slot, blk_expert.astype(I32), n_used.astype(I32).reshape(1), n_blocks


def _dispatch_body(s0_ref, s1_ref, x_ref, xs_in_ref, xs_ref, sem):
    del xs_in_ref
    rows = x_ref.shape[0]

    def copy(r, slot):
        return pltpu.make_async_copy(x_ref.at[pl.ds(r, 1)], xs_ref.at[pl.ds(slot, 1)], sem)

    def issue(r, carry):
        copy(r, s0_ref[0, 0, r]).start()
        copy(r, s1_ref[0, 0, r]).start()
        return carry

    lax.fori_loop(0, rows, issue, 0, unroll=DMA_UNROLL)

    def drain(r, carry):
        copy(0, 0).wait()
        copy(0, 0).wait()
        return carry

    lax.fori_loop(0, rows, drain, 0, unroll=DMA_UNROLL)


def _dispatch(xp, slot, n_slots):
    n, w = xp.shape
    tr = ROUTE_ROWS
    s0 = slot[:, 0].reshape(n // tr, 1, tr)
    s1 = slot[:, 1].reshape(n // tr, 1, tr)
    smem = pl.BlockSpec((1, 1, tr), lambda i: (i, 0, 0), memory_space=pltpu.SMEM)
    return pl.pallas_call(
        _dispatch_body,
        grid=(n // tr,),
        in_specs=[smem, smem, pl.BlockSpec((tr, w), lambda i: (i, 0)),
                  pl.BlockSpec(memory_space=pl.ANY)],
        out_specs=pl.BlockSpec(memory_space=pl.ANY),
        out_shape=jax.ShapeDtypeStruct((n_slots, w), xp.dtype),
        scratch_shapes=[pltpu.SemaphoreType.DMA(())],
        input_output_aliases={3: 0},
        compiler_params=_params("arbitrary"),
        name="moe_dispatch",
    )(s0, s1, xp, jnp.zeros((n_slots, w), xp.dtype))


def _moe_body(be_ref, nu_ref, x_ref, wg_ref, wu_ref, wd_ref, y_ref, xb_ref):
    del be_ref
    b = pl.program_id(0)
    f = pl.program_id(1)

    @pl.when(f == 0)
    def _():
        y_ref[...] = jnp.zeros_like(y_ref)
        packed = x_ref[...]
        half = packed.shape[1]
        xb_ref[:, :half] = lax.bitcast_convert_type(packed << 16, F32).astype(BF16)
        xb_ref[:, half:] = lax.bitcast_convert_type(
            packed & jnp.uint32(0xFFFF0000), F32).astype(BF16)

    @pl.when(b < nu_ref[0])
    def _():
        x = xb_ref[...]
        gate = jnp.dot(x, wg_ref[...].astype(BF16), preferred_element_type=F32)
        up = jnp.dot(x, wu_ref[...].astype(BF16), preferred_element_type=F32)
        act = (gate * _sigmoid(gate) * up).astype(BF16)
        y_ref[...] += jnp.dot(act, wd_ref[...].astype(BF16), preferred_element_type=F32)


def _moe_experts(xs, blk_expert, n_used, wg, wu, wd, layer, n_blocks):
    n_slots, half = xs.shape
    c = 2 * half
    ff = wg.shape[3]
    tb, tf = MOE_ROWS, MOE_FF_TILE
    nf = ff // tf

    def used(b, nu):
        return jnp.minimum(b, nu[0] - 1)

    def f_of(b, f, nu):
        return jnp.where(b < nu[0], f, nf - 1)

    grid_spec = pltpu.PrefetchScalarGridSpec(
        num_scalar_prefetch=2,
        grid=(n_blocks, nf),
        in_specs=[
            pl.BlockSpec((tb, half), lambda b, f, be, nu: (used(b, nu), 0)),
            pl.BlockSpec((None, None, c, tf),
                         lambda b, f, be, nu: (layer, be[used(b, nu)], 0, f_of(b, f, nu))),
            pl.BlockSpec((None, None, c, tf),
                         lambda b, f, be, nu: (layer, be[used(b, nu)], 0, f_of(b, f, nu))),
            pl.BlockSpec((None, None, tf, c),
                         lambda b, f, be, nu: (layer, be[used(b, nu)], f_of(b, f, nu), 0)),
        ],
        out_specs=pl.BlockSpec((tb, c), lambda b, f, be, nu: (b, 0)),
        scratch_shapes=[pltpu.VMEM((tb, c), BF16)],
    )
    return pl.pallas_call(
        _moe_body,
        grid_spec=grid_spec,
        out_shape=jax.ShapeDtypeStruct((n_slots, c), F32),
        compiler_params=_params("arbitrary", "arbitrary"),
        name="moe_experts",
    )(blk_expert, n_used, xs, wg, wu, wd)


def _combine_body(s0_ref, s1_ref, route_ref, h_ref, g_ref, ys_ref, out_ref, buf_ref, sem, *,
                  final_norm):
    rows = h_ref.shape[0]

    def copy(r, slot, k):
        return pltpu.make_async_copy(ys_ref.at[pl.ds(slot, 1)], buf_ref.at[k, pl.ds(r, 1)], sem)

    def issue(r, carry):
        copy(r, s0_ref[0, 0, r], 0).start()
        copy(r, s1_ref[0, 0, r], 1).start()
        return carry

    lax.fori_loop(0, rows, issue, 0, unroll=DMA_UNROLL)

    def drain(r, carry):
        copy(0, 0, 0).wait()
        copy(0, 0, 1).wait()
        return carry

    lax.fori_loop(0, rows, drain, 0, unroll=DMA_UNROLL)
    route = route_ref[...]
    out = h_ref[...] + buf_ref[0] * route[:, 2:3] + buf_ref[1] * route[:, 3:4]
    if final_norm:
        out = _rms_norm(out, g_ref[...])
    out_ref[...] = out


def _combine(ys, slot, route, h, g_final, final_norm):
    n, c = h.shape
    tr = ROUTE_ROWS
    s0 = slot[:, 0].reshape(n // tr, 1, tr)
    s1 = slot[:, 1].reshape(n // tr, 1, tr)
    smem = pl.BlockSpec((1, 1, tr), lambda i: (i, 0, 0), memory_space=pltpu.SMEM)
    return pl.pallas_call(
        functools.partial(_combine_body, final_norm=final_norm),
        grid=(n // tr,),
        in_specs=[smem, smem, pl.BlockSpec((tr, LANES), lambda i: (i, 0)),
                  pl.BlockSpec((tr, c), lambda i: (i, 0)), pl.BlockSpec((1, c), lambda i: (0, 0)),
                  pl.BlockSpec(memory_space=pl.ANY)],
        out_specs=pl.BlockSpec((tr, c), lambda i: (i, 0)),
        out_shape=jax.ShapeDtypeStruct((n, c), F32),
        scratch_shapes=[pltpu.VMEM((TOP_K, tr, c), F32), pltpu.SemaphoreType.DMA(())],
        compiler_params=_params("arbitrary"),
        name="moe_combine_norm" if final_norm else "moe_combine",
    )(s0, s1, route, h, g_final, ys)


def _cast_body(w_ref, o_ref):
    o_ref[...] = w_ref[...].astype(o_ref.dtype)


def _layer_bf16(w, layer):
    cols = w.shape[-1]
    w3 = w.reshape(w.shape[0], -1, cols)
    rows = w3.shape[1]
    blk = 1 << ((CAST_BLOCK_BYTES // (cols * 4)).bit_length() - 1)
    assert rows % blk == 0, (rows, blk)
    out = pl.pallas_call(
        _cast_body,
        grid=(rows // blk,),
        in_specs=[pl.BlockSpec((None, blk, cols), lambda i: (layer, i, 0))],
        out_specs=pl.BlockSpec((blk, cols), lambda i: (i, 0)),
        out_shape=jax.ShapeDtypeStruct((rows, cols), BF16),
        compiler_params=_params("parallel"),
        name="cast_bf16",
    )(w3)
    return out.reshape(w.shape[1:])


def _pad_cols(w, mult=LANES):
    pad = -w.shape[1] % mult
    return jnp.pad(w, ((0, 0), (0, pad))) if pad else w


def _pad_rows(w, mult=LANES):
    pad = -w.shape[0] % mult
    return jnp.pad(w, ((0, pad), (0, 0))) if pad else w


def kernel(x, norm_mix, norm_ffn, norm_final, pool_w, pool_scale, rwkv_mu, rwkv_w0, rwkv_w1, rwkv_w2, rwkv_a0, rwkv_a1, rwkv_a2, rwkv_v0, rwkv_v1, rwkv_v2, rwkv_g1, rwkv_g2, rwkv_k_k, rwkv_k_a, rwkv_r_k, rwkv_wr, rwkv_wk, rwkv_wv, rwkv_wo, rwkv_lnx_w, rwkv_lnx_b, ffn_w_gate, ffn_w_up, ffn_w_down, moe_router, moe_w_gate, moe_w_up, moe_w_down):
    batch, seq, c = x.shape
    n = batch * seq
    depth = norm_mix.shape[0]
    vec = lambda a: a.reshape(1, c)
    h = x.reshape(n, c)
    hn = None
    v_first = None
    for i in range(depth):
        j = i // 2
        if i % 2 == 0:
            h, hn_ffn = _pool_layer(h, seq, vec(norm_mix[i]), pool_w[j].astype(BF16),
                                    vec(pool_scale[j]), vec(norm_ffn[i]))
            h, hn = _ffn_layer(hn_ffn, h, _layer_bf16(ffn_w_gate, j), _layer_bf16(ffn_w_up, j),
                               _layer_bf16(ffn_w_down, j), vec(norm_mix[i + 1]))
        else:
            p = {
                "mu": jnp.pad(rwkv_mu[j], ((0, 2), (0, 0))),
                "wr": _layer_bf16(rwkv_wr, j), "wk": _layer_bf16(rwkv_wk, j),
                "wv": _layer_bf16(rwkv_wv, j),
                "w1": _pad_cols(rwkv_w1[j]).astype(BF16), "w2": _pad_rows(rwkv_w2[j]).astype(BF16),
                "a1": _pad_cols(rwkv_a1[j]).astype(BF16), "a2": _pad_rows(rwkv_a2[j]).astype(BF16),
                "g1": _pad_cols(rwkv_g1[j]).astype(BF16), "g2": _pad_rows(rwkv_g2[j]).astype(BF16),
                "w0": vec(rwkv_w0[j]), "a0": vec(rwkv_a0[j]),
                "k_k": vec(rwkv_k_k[j]), "k_a": vec(rwkv_k_a[j]),
            }
            if j > 0:
                p["v1"] = _pad_cols(rwkv_v1[j - 1]).astype(BF16)
                p["v2"] = _pad_rows(rwkv_v2[j - 1]).astype(BF16)
                p["v0"] = vec(rwkv_v0[j - 1])
            r, d, k, v, na, b, g = _rwkv_proj(hn, seq, p, v_first if j > 0 else None)
            if v_first is None:
                v_first = v
            y = _wkv(r, d, k, v, na, b, batch)
            h, hn_packed, route = _rwkv_out(
                y, r, k, v, g, h, vec(rwkv_lnx_w[j]), vec(rwkv_lnx_b[j]), vec(rwkv_r_k[j]),
                _layer_bf16(rwkv_wo, j), vec(norm_ffn[i]), _pad_cols(moe_router[j]))
            slot, blk_expert, n_used, n_blocks = _route_slots(route, MOE_ROWS)
            xs = _dispatch(hn_packed, slot, n_blocks * MOE_ROWS)
            ys = _moe_experts(xs, blk_expert, n_used, moe_w_gate, moe_w_up, moe_w_down, j, n_blocks)
            h = _combine(ys, slot, route, h, vec(norm_final), final_norm=(i == depth - 1))
    return h.reshape(batch, seq, c)
```

```python
import functools

import jax
import jax.numpy as jnp
from jax import lax
from jax.experimental import pallas as pl
from jax.experimental.pallas import tpu as pltpu

F32 = jnp.float32
BF16 = jnp.bfloat16
I32 = jnp.int32
U32 = jnp.uint32

LANES = 128
HEAD_DIM = 64
HEADS_PER_VREG = LANES // HEAD_DIM
RMS_EPS = 1e-6
GN_EPS = 64e-5
POOL_WINDOWS = (2, 4, 8, 16)
POOL_HALO = 16
N_EXPERTS = 8
TOP_K = 2
VMEM_LIMIT_BYTES = 56 * 1024 * 1024

ROW_TILE = 512
PROJ_ROWS = 512
OUT_ROWS = 256
COL_TILE = 512
FF_TILE = 1024
MOE_FF_TILE = 512
MOE_ROWS = 768
ROUTE_ROWS = 512
DMA_UNROLL = 8
CAST_BLOCK_BYTES = 8 * 1024 * 1024
WKV_CHUNK = 128
WKV_LANES = 256
WKV_TILE = 8


def _params(*sem):
    return pltpu.CompilerParams(dimension_semantics=sem, vmem_limit_bytes=VMEM_LIMIT_BYTES)


def _rms_norm(x, g):
    return x * lax.rsqrt(jnp.mean(x * x, axis=-1, keepdims=True) + RMS_EPS) * g


def _sigmoid(x):
    return 1.0 / (1.0 + jnp.exp(-x))


def _head_ones():
    r = lax.broadcasted_iota(I32, (LANES, LANES), 0) // HEAD_DIM
    c = lax.broadcasted_iota(I32, (LANES, LANES), 1) // HEAD_DIM
    return (r == c).astype(BF16)


def _head_sum(x, ones):
    outs = []
    for c in range(x.shape[1] // LANES):
        xc = x[:, c * LANES:(c + 1) * LANES]
        hi = xc.astype(BF16)
        lo = (xc - hi.astype(F32)).astype(BF16)
        outs.append(jnp.dot(hi, ones, preferred_element_type=F32)
                    + jnp.dot(lo, ones, preferred_element_type=F32))
    return outs[0] if len(outs) == 1 else jnp.concatenate(outs, axis=1)


def _pool_body(x_ref, halo_ref, gmix_ref, w_ref, scale_ref, gffn_ref, h_ref, hn_ref, *,
               tiles_per_seq):
    tm = x_ref.shape[0]
    group = w_ref.shape[1]
    tile_in_seq = pl.program_id(0) % tiles_per_seq
    x = x_ref[...]
    g = gmix_ref[...]
    xn = _rms_norm(x, g)
    keep = (tile_in_seq > 0).astype(F32)
    cat = jnp.concatenate([_rms_norm(halo_ref[...], g) * keep, xn], axis=0)
    row = lax.broadcasted_iota(I32, (tm, 1), 0)
    pos = (tile_in_seq * tm + row + 1).astype(F32)
    for gi, win in enumerate(POOL_WINDOWS):
        lo = gi * group
        s = cat[:, lo:lo + group]
        k = 1
        while k < win:
            s = s + pltpu.roll(s, k, 0)
            k *= 2
        pooled = s[POOL_HALO:, :] / jnp.minimum(pos, float(win))
        d = (pooled - xn[:, lo:lo + group]).astype(BF16)
        mixed = jnp.dot(d, w_ref[gi], preferred_element_type=F32)
        h_ref[:, lo:lo + group] = x[:, lo:lo + group] + mixed * scale_ref[:, lo:lo + group]
    hn_ref[...] = _rms_norm(h_ref[...], gffn_ref[...]).astype(BF16)


def _pool_layer(h, seq, g_mix, w, scale, g_ffn):
    n, c = h.shape
    tm = ROW_TILE
    halo_blocks = tm // POOL_HALO
    return pl.pallas_call(
        functools.partial(_pool_body, tiles_per_seq=seq // tm),
        grid=(n // tm,),
        in_specs=[
            pl.BlockSpec((tm, c), lambda i: (i, 0)),
            pl.BlockSpec((POOL_HALO, c), lambda i: (jnp.maximum(i * halo_blocks - 1, 0), 0)),
            pl.BlockSpec((1, c), lambda i: (0, 0)),
            pl.BlockSpec(w.shape, lambda i: (0, 0, 0)),
            pl.BlockSpec((1, c), lambda i: (0, 0)),
            pl.BlockSpec((1, c), lambda i: (0, 0)),
        ],
        out_specs=[pl.BlockSpec((tm, c), lambda i: (i, 0)), pl.BlockSpec((tm, c), lambda i: (i, 0))],
        out_shape=[jax.ShapeDtypeStruct((n, c), F32), jax.ShapeDtypeStruct((n, c), BF16)],
        compiler_params=_params("parallel"),
        name="pool_mixer",
    )(h, h, g_mix, w, scale, g_ffn)


def _ffn_body(x_ref, wg_ref, wu_ref, wd_ref, h_ref, gnext_ref, out_ref, hn_ref):
    f = pl.program_id(1)

    @pl.when(f == 0)
    def _():
        out_ref[...] = h_ref[...]

    x = x_ref[...]
    gate = jnp.dot(x, wg_ref[...], preferred_element_type=F32)
    up = jnp.dot(x, wu_ref[...], preferred_element_type=F32)
    act = (gate * _sigmoid(gate) * up).astype(BF16)
    out_ref[...] += jnp.dot(act, wd_ref[...], preferred_element_type=F32)

    @pl.when(f == pl.num_programs(1) - 1)
    def _():
        hn_ref[...] = _rms_norm(out_ref[...], gnext_ref[...])


def _ffn_layer(hn, h, wg, wu, wd, g_next):
    n, c = h.shape
    ff = wg.shape[1]
    tm, tf = ROW_TILE, FF_TILE
    return pl.pallas_call(
        _ffn_body,
        grid=(n // tm, ff // tf),
        in_specs=[
            pl.BlockSpec((tm, c), lambda i, f: (i, 0)),
            pl.BlockSpec((c, tf), lambda i, f: (0, f)),
            pl.BlockSpec((c, tf), lambda i, f: (0, f)),
            pl.BlockSpec((tf, c), lambda i, f: (f, 0)),
            pl.BlockSpec((tm, c), lambda i, f: (i, 0), pipeline_mode=pl.Buffered(1)),
            pl.BlockSpec((1, c), lambda i, f: (0, 0)),
        ],
        out_specs=[pl.BlockSpec((tm, c), lambda i, f: (i, 0)),
                   pl.BlockSpec((tm, c), lambda i, f: (i, 0))],
        out_shape=[jax.ShapeDtypeStruct((n, c), F32), jax.ShapeDtypeStruct((n, c), F32)],
        compiler_params=_params("parallel", "arbitrary"),
        name="dense_swiglu",
    )(hn, wg, wu, wd, h, g_next)


def _rwkv_proj_body(*refs, tiles_per_seq, has_vres):
    if has_vres:
        (hn_ref, halo_ref, mu_ref, wr_ref, wk_ref, wv_ref, w1_ref, a1_ref, g1_ref,
         w2_ref, a2_ref, g2_ref, w0_ref, a0_ref, kk_ref, ka_ref,
         v1_ref, v2_ref, v0_ref, vf_ref,
         r_out, d_out, k_out, v_out, na_out, b_out, g_out,
         xs_ref, hw_ref, ha_ref, hg_ref, hv_ref) = refs
    else:
        (hn_ref, halo_ref, mu_ref, wr_ref, wk_ref, wv_ref, w1_ref, a1_ref, g1_ref,
         w2_ref, a2_ref, g2_ref, w0_ref, a0_ref, kk_ref, ka_ref,
         r_out, d_out, k_out, v_out, na_out, b_out, g_out,
         xs_ref, hw_ref, ha_ref, hg_ref) = refs
    tm = hn_ref.shape[0]

    @pl.when(pl.program_id(1) == 0)
    def _():
        hn = hn_ref[...]
        keep = (pl.program_id(0) % tiles_per_seq > 0).astype(F32)
        before = halo_ref[halo_ref.shape[0] - 1:, :] * keep
        row = lax.broadcasted_iota(I32, (tm, 1), 0)
        prev = jnp.where(row == 0, before, pltpu.roll(hn, 1, 0))
        xx = prev - hn
        for m in range(6):
            xs_ref[m] = (hn + xx * mu_ref[m:m + 1, :]).astype(BF16)
        hw_ref[...] = jnp.tanh(jnp.dot(xs_ref[1], w1_ref[...], preferred_element_type=F32)).astype(BF16)
        ha_ref[...] = jnp.dot(xs_ref[4], a1_ref[...], preferred_element_type=F32).astype(BF16)
        hg_ref[...] = _sigmoid(jnp.dot(xs_ref[5], g1_ref[...], preferred_element_type=F32)).astype(BF16)
        if has_vres:
            hv_ref[...] = jnp.dot(xs_ref[3], v1_ref[...], preferred_element_type=F32).astype(BF16)

    wl = jnp.dot(hw_ref[...], w2_ref[...], preferred_element_type=F32)
    z = -(w0_ref[...] + wl)
    softplus = jnp.maximum(z, 0.0) + jnp.log(1.0 + jnp.exp(-jnp.abs(z)))
    d_out[...] = -jnp.exp(-softplus - 0.5)
    al = jnp.dot(ha_ref[...], a2_ref[...], preferred_element_type=F32)
    a = _sigmoid(a0_ref[...] + al)
    k = jnp.dot(xs_ref[2], wk_ref[...], preferred_element_type=F32)
    kk = k * kk_ref[...]
    norm = jnp.sqrt(_head_sum(kk * kk, _head_ones()))
    kk = kk / jnp.maximum(norm, 1e-12)
    k_out[...] = (k * (1.0 + (a - 1.0) * ka_ref[...])).astype(k_out.dtype)
    na_out[...] = (-kk).astype(na_out.dtype)
    b_out[...] = (kk * a).astype(b_out.dtype)
    v = jnp.dot(xs_ref[3], wv_ref[...], preferred_element_type=F32)
    if has_vres:
        vl = jnp.dot(hv_ref[...], v2_ref[...], preferred_element_type=F32)
        v = v + (vf_ref[...].astype(F32) - v) * _sigmoid(v0_ref[...] + vl)
    v_out[...] = v.astype(v_out.dtype)
    gl = jnp.dot(hg_ref[...], g2_ref[...], preferred_element_type=F32)
    g_out[...] = gl.astype(g_out.dtype)
    r = jnp.dot(xs_ref[0], wr_ref[...], preferred_element_type=F32)
    r_out[...] = r.astype(r_out.dtype)


def _rwkv_proj(hn, seq, p, v_first):
    n, c = hn.shape
    tm, tn = PROJ_ROWS, COL_TILE
    has_vres = v_first is not None
    halo_rows = 8
    row_spec = pl.BlockSpec((tm, c), lambda i, j: (i, 0))
    halo_spec = pl.BlockSpec((halo_rows, c),
                             lambda i, j: (jnp.maximum(i * (tm // halo_rows) - 1, 0), 0))
    full = lambda a: pl.BlockSpec(a.shape, lambda i, j: (0,) * a.ndim)
    cols = lambda a: pl.BlockSpec((a.shape[0], tn), lambda i, j: (0, j))
    tile = pl.BlockSpec((tm, tn), lambda i, j: (i, j))
    args = [hn, hn, p["mu"], p["wr"], p["wk"], p["wv"], p["w1"], p["a1"], p["g1"],
            p["w2"], p["a2"], p["g2"], p["w0"], p["a0"], p["k_k"], p["k_a"]]
    specs = [row_spec, halo_spec, full(p["mu"]), cols(p["wr"]), cols(p["wk"]), cols(p["wv"]),
             full(p["w1"]), full(p["a1"]), full(p["g1"]),
             cols(p["w2"]), cols(p["a2"]), cols(p["g2"]),
             cols(p["w0"]), cols(p["a0"]), cols(p["k_k"]), cols(p["k_a"])]
    scratch = [pltpu.VMEM((6, tm, c), BF16), pltpu.VMEM((tm, p["w1"].shape[1]), BF16),
               pltpu.VMEM((tm, p["a1"].shape[1]), BF16), pltpu.VMEM((tm, p["g1"].shape[1]), BF16)]
    if has_vres:
        args += [p["v1"], p["v2"], p["v0"], v_first]
        specs += [full(p["v1"]), cols(p["v2"]), cols(p["v0"]), tile]
        scratch.append(pltpu.VMEM((tm, p["v1"].shape[1]), BF16))
    return pl.pallas_call(
        functools.partial(_rwkv_proj_body, tiles_per_seq=seq // tm, has_vres=has_vres),
        grid=(n // tm, c // tn),
        in_specs=specs,
        out_specs=[tile] * 7,
        out_shape=[jax.ShapeDtypeStruct((n, c), F32 if i == 1 else BF16) for i in range(7)],
        scratch_shapes=scratch,
        compiler_params=_params("parallel", "arbitrary"),
        name="rwkv_proj_vres" if has_vres else "rwkv_proj",
    )(*args)


def _wkv_body(r_ref, w_ref, k_ref, v_ref, na_ref, b_ref, y_ref, s_ref, sb_ref, lhs_ref, e_ref, u0_ref):
    nb, tc, c = r_ref.shape
    sub = WKV_TILE
    pack = 2 * sub
    n_pairs = c // LANES
    n_groups = c // WKV_LANES
    kinds = 4

    @pl.when(pl.program_id(0) == 0)
    def _():
        s_ref[...] = jnp.zeros_like(s_ref)
        sb_ref[...] = jnp.zeros_like(sb_ref)

    head_r = lax.broadcasted_iota(I32, (WKV_LANES, WKV_LANES), 0) // HEAD_DIM
    head_c = lax.broadcasted_iota(I32, (WKV_LANES, WKV_LANES), 1) // HEAD_DIM
    ones = (head_r == head_c).astype(BF16)
    same_head = (lax.broadcasted_iota(I32, (LANES, LANES), 0) // HEAD_DIM
                 == lax.broadcasted_iota(I32, (LANES, LANES), 1) // HEAD_DIM)
    step = lax.broadcasted_iota(I32, (sub, c), 0)

    def shift_down(x, d):
        return jnp.where(step >= d, pltpu.roll(x, d, 0), 0.0)

    def spread(x, s):
        return jnp.broadcast_to(x[s:s + 1, :], (sub, c))

    def dots(bb, s, kind):
        start = (s * kinds + kind) * sub
        return e_ref[bb, start:start + sub, :]

    def solve_tile(t_pack, half):
        t0 = pl.multiple_of(t_pack + half * sub, sub)
        rows = slice(half * sub, (half + 1) * sub)
        batches = range(nb)
        decay, at, rt, bt, kt, v, vs, u, us = ([None] * nb for _ in range(9))
        for bb in batches:
            load = lambda ref: ref[bb, pl.ds(t_pack, pack), :].astype(F32)[rows, :]
            w = w_ref[bb, pl.ds(t0, sub), :]
            cum = w
            for d in (1, 2, 4):
                cum = cum + shift_down(cum, d)
            grow = jnp.exp(-cum)
            decay[bb] = jnp.exp(cum)
            at[bb] = load(na_ref) * jnp.exp(cum - w)
            rt[bb] = load(r_ref) * decay[bb]
            bt[bb] = load(b_ref) * grow
            kt[bb] = load(k_ref) * grow
            v[bb] = load(v_ref)
            for s in range(sub):
                bs, ks = spread(bt[bb], s), spread(kt[bb], s)
                base = s * kinds * sub
                lhs_ref[bb, base:base + 2 * sub, :] = jnp.concatenate(
                    [at[bb] * bs, at[bb] * ks], axis=0).astype(BF16)
                lhs_ref[bb, base + 2 * sub:base + 4 * sub, :] = jnp.concatenate(
                    [rt[bb] * bs, rt[bb] * ks], axis=0).astype(BF16)
            for g in range(n_groups):
                gl = slice(g * WKV_LANES, (g + 1) * WKV_LANES)
                e_ref[bb, :, gl] = jnp.dot(lhs_ref[bb, :, gl], ones, preferred_element_type=F32)
        for bb in batches:
            ar = jnp.concatenate([at[bb], rt[bb]], axis=0).astype(BF16)
            for p in range(n_pairs):
                pl_ = slice(p * LANES, (p + 1) * LANES)
                u0_ref[bb, :, pl_] = lax.dot_general(
                    ar[:, pl_], sb_ref[bb * n_pairs + p], (((1,), (1,)), ((), ())),
                    preferred_element_type=F32)
        for bb in batches:
            vs[bb] = [spread(v[bb], s) for s in range(sub)]
            u[bb] = u0_ref[bb, 0:sub, :]
            us[bb] = []
            for s in range(sub):
                us[bb].append(spread(u[bb], s))
                if s < sub - 1:
                    u[bb] = u[bb] + jnp.where(
                        step > s, dots(bb, s, 0) * us[bb][s] + dots(bb, s, 1) * vs[bb][s], 0.0)
        for bb in batches:
            uv = jnp.concatenate([u[bb], v[bb]], axis=0).astype(BF16)
            bk = jnp.concatenate([bt[bb], kt[bb]], axis=0).astype(BF16)
            for p in range(n_pairs):
                pl_ = slice(p * LANES, (p + 1) * LANES)
                idx = bb * n_pairs + p
                upd = lax.dot_general(uv[:, pl_], bk[:, pl_], (((0,), (0,)), ((), ())),
                                      preferred_element_type=F32)
                new = jnp.where(same_head, s_ref[idx] + upd, 0.0) * decay[bb][sub - 1:sub, pl_]
                s_ref[idx] = new
                sb_ref[idx] = new.astype(BF16)
        for bb in batches:
            y = u0_ref[bb, sub:2 * sub, :]
            for s in range(sub):
                y = y + jnp.where(
                    step >= s, dots(bb, s, 2) * us[bb][s] + dots(bb, s, 3) * vs[bb][s], 0.0)
            y_ref[bb, pl.ds(t0, sub), :] = y

    def pack_steps(i, carry):
        t_pack = pl.multiple_of(i * pack, pack)
        for half in range(pack // sub):
            solve_tile(t_pack, half)
        return carry

    lax.fori_loop(0, tc // pack, pack_steps, 0)


def _wkv(r, w, k, v, na, b, batch):
    n, c = r.shape
    seq = n // batch
    tc = WKV_CHUNK
    n_pairs = c // LANES
    shaped = [a.reshape(batch, seq, c) for a in (r, w, k, v, na, b)]
    spec = pl.BlockSpec((batch, tc, c), lambda i: (0, i, 0))
    y = pl.pallas_call(
        _wkv_body,
        grid=(seq // tc,),
        in_specs=[spec] * 6,
        out_specs=spec,
        out_shape=jax.ShapeDtypeStruct((batch, seq, c), F32),
        scratch_shapes=[pltpu.VMEM((batch * n_pairs, LANES, LANES), F32),
                        pltpu.VMEM((batch * n_pairs, LANES, LANES), BF16),
                        pltpu.VMEM((batch, 4 * WKV_TILE * WKV_TILE, c), BF16),
                        pltpu.VMEM((batch, 4 * WKV_TILE * WKV_TILE, c), F32),
                        pltpu.VMEM((batch, 2 * WKV_TILE, c), F32)],
        compiler_params=_params("arbitrary"),
        name="wkv7_scan",
    )(*shaped)
    return y.reshape(n, c)


def _rwkv_out_body(y_ref, r_ref, k_ref, v_ref, g_ref, h_ref, lnw_ref, lnb_ref, rk_ref, wo_ref,
                   gffn_ref, router_ref, out_ref, hnp_ref, route_ref, z_ref):
    j = pl.program_id(1)
    tn = wo_ref.shape[1]

    @pl.when(j == 0)
    def _():
        ones = _head_ones()
        y = y_ref[...]
        mean = _head_sum(y, ones) * (1.0 / HEAD_DIM)
        yc = y - mean
        var = _head_sum(yc * yc, ones) * (1.0 / HEAD_DIM)
        yn = yc * lax.rsqrt(var + GN_EPS) * lnw_ref[...] + lnb_ref[...]
        rk = r_ref[...].astype(F32) * k_ref[...].astype(F32) * rk_ref[...]
        bonus = _head_sum(rk, ones) * v_ref[...].astype(F32)
        z_ref[...] = ((yn + bonus) * g_ref[...].astype(F32)).astype(BF16)

    col = pl.multiple_of(j * tn, tn)
    out_ref[:, pl.ds(col, tn)] = h_ref[:, pl.ds(col, tn)] + jnp.dot(
        z_ref[...], wo_ref[...], preferred_element_type=F32)

    @pl.when(j == pl.num_programs(1) - 1)
    def _():
        hn = _rms_norm(out_ref[...], gffn_ref[...])
        half = hn.shape[1] // 2
        lo_bits = lax.bitcast_convert_type(hn[:, :half].astype(BF16).astype(F32), U32) >> 16
        hi_bits = lax.bitcast_convert_type(hn[:, half:].astype(BF16).astype(F32), U32)
        hnp_ref[...] = (hi_bits & jnp.uint32(0xFFFF0000)) | lo_bits
        logits = jnp.dot(hn, router_ref[...], preferred_element_type=F32,
                         precision=lax.Precision.HIGHEST)
        lane = lax.broadcasted_iota(I32, logits.shape, 1)
        neg = jnp.float32(-jnp.inf)
        lg = jnp.where(lane < N_EXPERTS, logits, neg)
        m1 = jnp.max(lg, axis=1, keepdims=True)
        i1 = jnp.min(jnp.where(lg == m1, lane, LANES), axis=1, keepdims=True)
        lg2 = jnp.where(lane == i1, neg, lg)
        m2 = jnp.max(lg2, axis=1, keepdims=True)
        i2 = jnp.min(jnp.where(lg2 == m2, lane, LANES), axis=1, keepdims=True)
        e = jnp.exp(m2 - m1)
        g1 = 1.0 / (1.0 + e)
        g2 = e / (1.0 + e)
        route_ref[...] = jnp.where(
            lane == 0, i1.astype(F32),
            jnp.where(lane == 1, i2.astype(F32),
                      jnp.where(lane == 2, g1, jnp.where(lane == 3, g2, 0.0))))


def _rwkv_out(y, r, k, v, g, h, lnw, lnb, rk, wo, g_ffn, router):
    n, c = h.shape
    tm, tn = OUT_ROWS, COL_TILE
    row = pl.BlockSpec((tm, c), lambda i, j: (i, 0))
    vec = pl.BlockSpec((1, c), lambda i, j: (0, 0))
    return pl.pallas_call(
        _rwkv_out_body,
        grid=(n // tm, c // tn),
        in_specs=[row, row, row, row, row, row, vec, vec, vec,
                  pl.BlockSpec((c, tn), lambda i, j: (0, j)), vec,
                  pl.BlockSpec(router.shape, lambda i, j: (0, 0))],
        out_specs=[row, pl.BlockSpec((tm, c // 2), lambda i, j: (i, 0)),
                   pl.BlockSpec((tm, LANES), lambda i, j: (i, 0))],
        out_shape=[jax.ShapeDtypeStruct((n, c), F32), jax.ShapeDtypeStruct((n, c // 2), U32),
                   jax.ShapeDtypeStruct((n, LANES), F32)],
        scratch_shapes=[pltpu.VMEM((tm, c), BF16)],
        compiler_params=_params("parallel", "arbitrary"),
        name="rwkv_out_router",
    )(y, r, k, v, g, h, lnw, lnb, rk, wo, g_ffn, router)


def _route_slots(route, rows_per_block):
    n = route.shape[0]
    experts = route[:, :TOP_K].astype(I32)
    flat = experts.reshape(-1)
    onehot = (flat[None, :] == jnp.arange(N_EXPERTS, dtype=I32)[:, None]).astype(I32)
    csum = jnp.cumsum(onehot, axis=1)
    counts = csum[:, -1]
    padded = (counts + rows_per_block - 1) // rows_per_block * rows_per_block
    pad_end = jnp.cumsum(padded)
    pad_start = pad_end - padded
    slot = jnp.sum(onehot * (csum - 1 + pad_start[:, None]), axis=0).reshape(n, TOP_K)
    n_blocks = -(-(n * TOP_K) // rows_per_block) + N_EXPERTS
    blk_start = jnp.arange(n_blocks, dtype=I32) * rows_per_block
    blk_expert = jnp.minimum(jnp.sum((blk_start[:, None] >= pad_end[None, :]).astype(I32), axis=1),
                             N_EXPERTS - 1)
    n_used = pad_end[-1] // rows_per_block
    return slot, blk_expert.astype(I32), n_used.astype(I32).reshape(1), n_blocks


def _dispatch_body(s0_ref, s1_ref, x_ref, xs_in_ref, xs_ref, sem):
    del xs_in_ref
    rows = x_ref.shape[0]

    def copy(r, slot):
        return pltpu.make_async_copy(x_ref.at[pl.ds(r, 1)], xs_ref.at[pl.ds(slot, 1)], sem)

    def issue(r, carry):
        copy(r, s0_ref[0, 0, r]).start()
        copy(r, s1_ref[0, 0, r]).start()
        return carry

    lax.fori_loop(0, rows, issue, 0, unroll=DMA_UNROLL)

    def drain(r, carry):
        copy(0, 0).wait()
        copy(0, 0).wait()
        return carry

    lax.fori_loop(0, rows, drain, 0, unroll=DMA_UNROLL)


def _dispatch(xp, slot, n_slots):
    n, w = xp.shape
    tr = ROUTE_ROWS
    s0 = slot[:, 0].reshape(n // tr, 1, tr)
    s1 = slot[:, 1].reshape(n // tr, 1, tr)
    smem = pl.BlockSpec((1, 1, tr), lambda i: (i, 0, 0), memory_space=pltpu.SMEM)
    return pl.pallas_call(
        _dispatch_body,
        grid=(n // tr,),
        in_specs=[smem, smem, pl.BlockSpec((tr, w), lambda i: (i, 0)),
                  pl.BlockSpec(memory_space=pl.ANY)],
        out_specs=pl.BlockSpec(memory_space=pl.ANY),
        out_shape=jax.ShapeDtypeStruct((n_slots, w), xp.dtype),
        scratch_shapes=[pltpu.SemaphoreType.DMA(())],
        input_output_aliases={3: 0},
        compiler_params=_params("arbitrary"),
        name="moe_dispatch",
    )(s0, s1, xp, jnp.zeros((n_slots, w), xp.dtype))


def _moe_body(be_ref, nu_ref, x_ref, wg_ref, wu_ref, wd_ref, y_ref, xb_ref):
    del be_ref
    b = pl.program_id(0)
    f = pl.program_id(1)

    @pl.when(f == 0)
    def _():
        y_ref[...] = jnp.zeros_like(y_ref)
        packed = x_ref[...]
        half = packed.shape[1]
        xb_ref[:, :half] = lax.bitcast_convert_type(packed << 16, F32).astype(BF16)
        xb_ref[:, half:] = lax.bitcast_convert_type(
            packed & jnp.uint32(0xFFFF0000), F32).astype(BF16)

    @pl.when(b < nu_ref[0])
    def _():
        x = xb_ref[...]
        gate = jnp.dot(x, wg_ref[...].astype(BF16), preferred_element_type=F32)
        up = jnp.dot(x, wu_ref[...].astype(BF16), preferred_element_type=F32)
        act = (gate * _sigmoid(gate) * up).astype(BF16)
        y_ref[...] += jnp.dot(act, wd_ref[...].astype(BF16), preferred_element_type=F32)


def _moe_experts(xs, blk_expert, n_used, wg, wu, wd, layer, n_blocks):
    n_slots, half = xs.shape
    c = 2 * half
    ff = wg.shape[3]
    tb, tf = MOE_ROWS, MOE_FF_TILE
    nf = ff // tf

    def used(b, nu):
        return jnp.minimum(b, nu[0] - 1)

    def f_of(b, f, nu):
        return jnp.where(b < nu[0], f, nf - 1)

    grid_spec = pltpu.PrefetchScalarGridSpec(
        num_scalar_prefetch=2,
        grid=(n_blocks, nf),
        in_specs=[
            pl.BlockSpec((tb, half), lambda b, f, be, nu: (used(b, nu), 0)),
            pl.BlockSpec((None, None, c, tf),
                         lambda b, f, be, nu: (layer, be[used(b, nu)], 0, f_of(b, f, nu))),
            pl.BlockSpec((None, None, c, tf),
                         lambda b, f, be, nu: (layer, be[used(b, nu)], 0, f_of(b, f, nu))),
            pl.BlockSpec((None, None, tf, c),
                         lambda b, f, be, nu: (layer, be[used(b, nu)], f_of(b, f, nu), 0)),
        ],
        out_specs=pl.BlockSpec((tb, c), lambda b, f, be, nu: (b, 0)),
        scratch_shapes=[pltpu.VMEM((tb, c), BF16)],
    )
    return pl.pallas_call(
        _moe_body,
        grid_spec=grid_spec,
        out_shape=jax.ShapeDtypeStruct((n_slots, c), F32),
        compiler_params=_params("arbitrary", "arbitrary"),
        name="moe_experts",
    )(blk_expert, n_used, xs, wg, wu, wd)


def _combine_body(s0_ref, s1_ref, route_ref, h_ref, g_ref, ys_ref, out_ref, buf_ref, sem, *,
                  final_norm):
    rows = h_ref.shape[0]

    def copy(r, slot, k):
        return pltpu.make_async_copy(ys_ref.at[pl.ds(slot, 1)], buf_ref.at[k, pl.ds(r, 1)], sem)

    def issue(r, carry):
        copy(r, s0_ref[0, 0, r], 0).start()
        copy(r, s1_ref[0, 0, r], 1).start()
        return carry

    lax.fori_loop(0, rows, issue, 0, unroll=DMA_UNROLL)

    def drain(r, carry):
        copy(0, 0, 0).wait()
        copy(0, 0, 1).wait()
        return carry

    lax.fori_loop(0, rows, drain, 0, unroll=DMA_UNROLL)
    route = route_ref[...]
    out = h_ref[...] + buf_ref[0] * route[:, 2:3] + buf_ref[1] * route[:, 3:4]
    if final_norm:
        out = _rms_norm(out, g_ref[...])
    out_ref[...] = out


def _combine(ys, slot, route, h, g_final, final_norm):
    n, c = h.shape
    tr = ROUTE_ROWS
    s0 = slot[:, 0].reshape(n // tr, 1, tr)
    s1 = slot[:, 1].reshape(n // tr, 1, tr)
    smem = pl.BlockSpec((1, 1, tr), lambda i: (i, 0, 0), memory_space=pltpu.SMEM)
    return pl.pallas_call(
        functools.partial(_combine_body, final_norm=final_norm),
        grid=(n // tr,),
        in_specs=[smem, smem, pl.BlockSpec((tr, LANES), lambda i: (i, 0)),
                  pl.BlockSpec((tr, c), lambda i: (i, 0)), pl.BlockSpec((1, c), lambda i: (0, 0)),
                  pl.BlockSpec(memory_space=pl.ANY)],
        out_specs=pl.BlockSpec((tr, c), lambda i: (i, 0)),
        out_shape=jax.ShapeDtypeStruct((n, c), F32),
        scratch_shapes=[pltpu.VMEM((TOP_K, tr, c), F32), pltpu.SemaphoreType.DMA(())],
        compiler_params=_params("arbitrary"),
        name="moe_combine_norm" if final_norm else "moe_combine",
    )(s0, s1, route, h, g_final, ys)


def _cast_body(w_ref, o_ref):
    o_ref[...] = w_ref[...].astype(o_ref.dtype)


def _layer_bf16(w, layer):
    cols = w.shape[-1]
    w3 = w.reshape(w.shape[0], -1, cols)
    rows = w3.shape[1]
    blk = 1 << ((CAST_BLOCK_BYTES // (cols * 4)).bit_length() - 1)
    assert rows % blk == 0, (rows, blk)
    out = pl.pallas_call(
        _cast_body,
        grid=(rows // blk,),
        in_specs=[pl.BlockSpec((None, blk, cols), lambda i: (layer, i, 0))],
        out_specs=pl.BlockSpec((blk, cols), lambda i: (i, 0)),
        out_shape=jax.ShapeDtypeStruct((rows, cols), BF16),
        compiler_params=_params("parallel"),
        name="cast_bf16",
    )(w3)
    return out.reshape(w.shape[1:])


def _pad_cols(w, mult=LANES):
    pad = -w.shape[1] % mult
    return jnp.pad(w, ((0, 0), (0, pad))) if pad else w


def _pad_rows(w, mult=LANES):
    pad = -w.shape[0] % mult
    return jnp.pad(w, ((0, pad), (0, 0))) if pad else w


def kernel(x, norm_mix, norm_ffn, norm_final, pool_w, pool_scale, rwkv_mu, rwkv_w0, rwkv_w1, rwkv_w2, rwkv_a0, rwkv_a1, rwkv_a2, rwkv_v0, rwkv_v1, rwkv_v2, rwkv_g1, rwkv_g2, rwkv_k_k, rwkv_k_a, rwkv_r_k, rwkv_wr, rwkv_wk, rwkv_wv, rwkv_wo, rwkv_lnx_w, rwkv_lnx_b, ffn_w_gate, ffn_w_up, ffn_w_down, moe_router, moe_w_gate, moe_w_up, moe_w_down):
    batch, seq, c = x.shape
    n = batch * seq
    depth = norm_mix.shape[0]
    vec = lambda a: a.reshape(1, c)
    h = x.reshape(n, c)
    hn = None
    v_first = None
    for i in range(depth):
        j = i // 2
        if i % 2 == 0:
            h, hn_ffn = _pool_layer(h, seq, vec(norm_mix[i]), pool_w[j].astype(BF16),
                                    vec(pool_scale[j]), vec(norm_ffn[i]))
            h, hn = _ffn_layer(hn_ffn, h, _layer_bf16(ffn_w_gate, j), _layer_bf16(ffn_w_up, j),
                               _layer_bf16(ffn_w_down, j), vec(norm_mix[i + 1]))
        else:
            p = {
                "mu": jnp.pad(rwkv_mu[j], ((0, 2), (0, 0))),
                "wr": _layer_bf16(rwkv_wr, j), "wk": _layer_bf16(rwkv_wk, j),
                "wv": _layer_bf16(rwkv_wv, j),
                "w1": _pad_cols(rwkv_w1[j]).astype(BF16), "w2": _pad_rows(rwkv_w2[j]).astype(BF16),
                "a1": _pad_cols(rwkv_a1[j]).astype(BF16), "a2": _pad_rows(rwkv_a2[j]).astype(BF16),
                "g1": _pad_cols(rwkv_g1[j]).astype(BF16), "g2": _pad_rows(rwkv_g2[j]).astype(BF16),
                "w0": vec(rwkv_w0[j]), "a0": vec(rwkv_a0[j]),
                "k_k": vec(rwkv_k_k[j]), "k_a": vec(rwkv_k_a[j]),
            }
            if j > 0:
                p["v1"] = _pad_cols(rwkv_v1[j - 1]).astype(BF16)
                p["v2"] = _pad_rows(rwkv_v2[j - 1]).astype(BF16)
                p["v0"] = vec(rwkv_v0[j - 1])
            r, d, k, v, na, b, g = _rwkv_proj(hn, seq, p, v_first if j > 0 else None)
            if v_first is None:
                v_first = v
            y = _wkv(r, d, k, v, na, b, batch)
            h, hn_packed, route = _rwkv_out(
                y, r, k, v, g, h, vec(rwkv_lnx_w[j]), vec(rwkv_lnx_b[j]), vec(rwkv_r_k[j]),
                _layer_bf16(rwkv_wo, j), vec(norm_ffn[i]), _pad_cols(moe_router[j]))
            slot, blk_expert, n_used, n_blocks = _route_slots(route, MOE_ROWS)
            xs = _dispatch(hn_packed, slot, n_blocks * MOE_ROWS)
            ys = _moe_experts(xs, blk_expert, n_used, moe_w_gate, moe_w_up, moe_w_down, j, n_blocks)
            h = _combine(ys, slot, route, h, vec(norm_final), final_norm=(i == depth - 1))
    return h.reshape(batch, seq, c)
```
